```python
import math
import jax, jax.numpy as jnp
from jax import lax
import numpy as np

D_MODEL = 1024
BATCH = 2
SEQ = 8192
DEPTH = 2

D_FF = 2816
EPS = 1e-6
Q_BLOCK = 128
NEG_INF = -1e30

MLA_HEADS = 8
MLA_NOPE_DIM = 64
MLA_ROPE_DIM = 32
MLA_V_DIM = 64
MLA_Q_RANK = 256
MLA_KV_RANK = 128
ROPE_BASE = 10000.0

DIFF_HEADS = 8
DIFF_HEAD_DIM = 32

POOL_WINDOWS = (2, 4, 8, 16)
POOL_GROUP = 128
POOL_WIDTH = POOL_GROUP * len(POOL_WINDOWS)

N_BRANCH = 3
BRANCH_WIDTH = 512

DIFF_QK = DIFF_HEADS * 2 * DIFF_HEAD_DIM
DIFF_V = DIFF_HEADS * 2 * DIFF_HEAD_DIM
GATE_WIDTH = N_BRANCH * D_MODEL
IN_SPLITS = (MLA_Q_RANK, MLA_KV_RANK, MLA_ROPE_DIM, DIFF_QK, DIFF_QK, DIFF_V, POOL_WIDTH, GATE_WIDTH)
IN_WIDTH = sum(IN_SPLITS)

kernel_name = "hybrid_mla_diffattn_pool_macaron"


def rmsnorm(x, g):
    xf = x.astype(jnp.float32)
    y = xf * lax.rsqrt(jnp.mean(xf * xf, axis=-1, keepdims=True) + EPS)
    return (y * g.astype(jnp.float32)).astype(x.dtype)


def swiglu(x, w_gate, w_up, w_down):
    return (jax.nn.silu(x @ w_gate) * (x @ w_up)) @ w_down


def rope(x, pos):
    half = x.shape[-1] // 2
    inv_freq = ROPE_BASE ** (-jnp.arange(half, dtype=jnp.float32) / half)
    ang = pos.astype(jnp.float32)[:, :, None, None] * inv_freq
    cos, sin = jnp.cos(ang), jnp.sin(ang)
    xf = x.astype(jnp.float32)
    x1, x2 = xf[..., :half], xf[..., half:]
    return jnp.concatenate([x1 * cos - x2 * sin, x2 * cos + x1 * sin], axis=-1).astype(x.dtype)


def sweep_query_blocks(fn, *qs):
    b, s = qs[0].shape[:2]
    nb = s // Q_BLOCK
    blocked = tuple(jnp.moveaxis(q.reshape((b, nb, Q_BLOCK) + q.shape[2:]), 1, 0) for q in qs)
    starts = jnp.arange(nb, dtype=jnp.int32) * Q_BLOCK
    out = lax.map(lambda a: fn(a[0], *a[1]), (starts, blocked))
    out = jnp.moveaxis(out, 0, 1)
    return out.reshape((b, s) + out.shape[3:])


def causal_mask(start, s):
    q_idx = start + jnp.arange(Q_BLOCK, dtype=jnp.int32)
    k_idx = jnp.arange(s, dtype=jnp.int32)
    return k_idx[None, :] <= q_idx[:, None]


def mla_mixer(c_q, c_kv, k_rope_raw, pos, q_norm, w_uq, kv_norm, w_ukv):
    b, s, _ = c_q.shape
    q = (rmsnorm(c_q, q_norm) @ w_uq).reshape(b, s, MLA_HEADS, MLA_NOPE_DIM + MLA_ROPE_DIM)
    q_nope, q_rope = q[..., :MLA_NOPE_DIM], rope(q[..., MLA_NOPE_DIM:], pos)
    kv = (rmsnorm(c_kv, kv_norm) @ w_ukv).reshape(b, s, MLA_HEADS, MLA_NOPE_DIM + MLA_V_DIM)
    k_nope, v = kv[..., :MLA_NOPE_DIM], kv[..., MLA_NOPE_DIM:]
    k_rope = rope(k_rope_raw[:, :, None, :], pos)[:, :, 0]
    scale = (MLA_NOPE_DIM + MLA_ROPE_DIM) ** -0.5

    def block(start, qn, qr):
        sc = jnp.einsum('bqhd,bkhd->bhqk', qn, k_nope) + jnp.einsum('bqhd,bkd->bhqk', qr, k_rope)
        sc = jnp.where(causal_mask(start, s), sc.astype(jnp.float32) * scale, NEG_INF)
        p = jax.nn.softmax(sc, axis=-1).astype(v.dtype)
        return jnp.einsum('bhqk,bkhd->bqhd', p, v)

    o = sweep_query_blocks(block, q_nope, q_rope)
    return o.reshape(b, s, MLA_HEADS * MLA_V_DIM)


def diff_mixer(q, k, v, pos, lq1, lk1, lq2, lk2, subln, lambda_init):
    b, s, _ = q.shape
    q = q.reshape(b, s, DIFF_HEADS, 2, DIFF_HEAD_DIM)
    k = k.reshape(b, s, DIFF_HEADS, 2, DIFF_HEAD_DIM)
    v = v.reshape(b, s, DIFF_HEADS, 2 * DIFF_HEAD_DIM)
    f32 = jnp.float32
    lam = (jnp.exp(jnp.sum(lq1.astype(f32) * lk1.astype(f32)))
           - jnp.exp(jnp.sum(lq2.astype(f32) * lk2.astype(f32))) + lambda_init)
    slopes = jnp.exp2(-8.0 * jnp.arange(1, DIFF_HEADS + 1, dtype=f32) / DIFF_HEADS)
    scale = DIFF_HEAD_DIM ** -0.5

    def block(start, qb, pb):
        sc = jnp.einsum('bqhmd,bkhmd->bmhqk', qb, k).astype(f32) * scale
        dist = jnp.abs(pb[:, :, None] - pos[:, None, :]).astype(f32)
        sc = sc - slopes[None, None, :, None, None] * dist[:, None, None]
        sc = jnp.where(causal_mask(start, s), sc, NEG_INF)
        p = jax.nn.softmax(sc, axis=-1)
        a = (p[:, 0] - lam * p[:, 1]).astype(v.dtype)
        return jnp.einsum('bhqk,bkhd->bqhd', a, v)

    o = sweep_query_blocks(block, q, pos)
    o = rmsnorm(o, subln) * (1.0 - lambda_init)
    return o.reshape(b, s, DIFF_HEADS * 2 * DIFF_HEAD_DIM)


def pool_mixer(p, pool_w, pool_b, pool_scale):
    b, s, _ = p.shape
    pf = p.astype(jnp.float32)
    cs = jnp.cumsum(pf, axis=1)
    t = jnp.arange(s, dtype=jnp.int32)
    outs = []
    for g, w in enumerate(POOL_WINDOWS):
        sl = slice(g * POOL_GROUP, (g + 1) * POOL_GROUP)
        c = cs[..., sl]
        prev = jnp.pad(c, ((0, 0), (w, 0), (0, 0)))[:, :s]
        cnt = jnp.minimum(t + 1, w).astype(jnp.float32)[None, :, None]
        outs.append((c - prev) / cnt - pf[..., sl])
    pooled = jnp.stack(outs, axis=2).astype(p.dtype)
    y = jnp.einsum('bsgc,gcd->bsgd', pooled, pool_w) + pool_b
    return y.reshape(b, s, POOL_WIDTH) * pool_scale


def setup_inputs(seed: int = 0) -> dict:
    key = jax.random.key(seed)
    ks = iter(jax.random.split(key, 40))
    f32 = jnp.float32

    def dense(shape, fan_in):
        return jax.random.normal(next(ks), shape, f32) * fan_in ** -0.5

    def gain(shape):
        return 1.0 + 0.02 * jax.random.normal(next(ks), shape, f32)

    x = jax.random.normal(next(ks), (BATCH, SEQ, D_MODEL), f32)
    offset = jax.random.randint(next(ks), (BATCH, 1), 0, 1024, dtype=jnp.int32)
    positions = (jnp.arange(SEQ, dtype=jnp.int32)[None, :] + offset).astype(jnp.int32)
    return {
        "x": x,
        "positions": positions,
        "ffn1_norm": gain((DEPTH, D_MODEL)),
        "ffn1_w_gate": dense((DEPTH, D_MODEL, D_FF), D_MODEL),
        "ffn1_w_up": dense((DEPTH, D_MODEL, D_FF), D_MODEL),
        "ffn1_w_down": dense((DEPTH, D_FF, D_MODEL), D_FF),
        "mix_norm": gain((DEPTH, D_MODEL)),
        "w_in": dense((DEPTH, D_MODEL, IN_WIDTH), D_MODEL),
        "mla_q_norm": gain((DEPTH, MLA_Q_RANK)),
        "mla_w_uq": dense((DEPTH, MLA_Q_RANK, MLA_HEADS * (MLA_NOPE_DIM + MLA_ROPE_DIM)), MLA_Q_RANK),
        "mla_kv_norm": gain((DEPTH, MLA_KV_RANK)),
        "mla_w_ukv": dense((DEPTH, MLA_KV_RANK, MLA_HEADS * (MLA_NOPE_DIM + MLA_V_DIM)), MLA_KV_RANK),
        "diff_lambda_q1": 0.1 * jax.random.normal(next(ks), (DEPTH, DIFF_HEAD_DIM), f32),
        "diff_lambda_k1": 0.1 * jax.random.normal(next(ks), (DEPTH, DIFF_HEAD_DIM), f32),
        "diff_lambda_q2": 0.1 * jax.random.normal(next(ks), (DEPTH, DIFF_HEAD_DIM), f32),
        "diff_lambda_k2": 0.1 * jax.random.normal(next(ks), (DEPTH, DIFF_HEAD_DIM), f32),
        "diff_subln": gain((DEPTH, 2 * DIFF_HEAD_DIM)),
        "pool_w": dense((DEPTH, len(POOL_WINDOWS), POOL_GROUP, POOL_GROUP), POOL_GROUP),
        "pool_b": 0.01 * jax.random.normal(next(ks), (DEPTH, len(POOL_WINDOWS), POOL_GROUP), f32),
        "pool_scale": 1.0 + 0.05 * jax.random.normal(next(ks), (DEPTH, POOL_WIDTH), f32),
        "w_branch": dense((DEPTH, N_BRANCH, BRANCH_WIDTH, D_MODEL), BRANCH_WIDTH),
        "w_out": dense((DEPTH, D_MODEL, D_MODEL), D_MODEL),
        "ffn2_norm": gain((DEPTH, D_MODEL)),
        "ffn2_w_gate": dense((DEPTH, D_MODEL, D_FF), D_MODEL),
        "ffn2_w_up": dense((DEPTH, D_MODEL, D_FF), D_MODEL),
        "ffn2_w_down": dense((DEPTH, D_FF, D_MODEL), D_FF),
        "final_norm": gain((D_MODEL,)),
    }


def reference(x, positions, ffn1_norm, ffn1_w_gate, ffn1_w_up, ffn1_w_down, mix_norm, w_in,
              mla_q_norm, mla_w_uq, mla_kv_norm, mla_w_ukv,
              diff_lambda_q1, diff_lambda_k1, diff_lambda_q2, diff_lambda_k2, diff_subln,
              pool_w, pool_b, pool_scale, w_branch, w_out,
              ffn2_norm, ffn2_w_gate, ffn2_w_up, ffn2_w_down, final_norm):
    b, s, d = x.shape
    split_idx = [int(v) for v in np.cumsum(IN_SPLITS)[:-1]]
    h = x
    for l in range(DEPTH):
        h = h + 0.5 * swiglu(rmsnorm(h, ffn1_norm[l]), ffn1_w_gate[l], ffn1_w_up[l], ffn1_w_down[l])

        u = rmsnorm(h, mix_norm[l])
        z = u @ w_in[l]
        c_q, c_kv, k_rope, dq, dk, dv, p_in, z_gate = jnp.split(z, split_idx, axis=-1)

        y_mla = mla_mixer(c_q, c_kv, k_rope, positions,
                          mla_q_norm[l], mla_w_uq[l], mla_kv_norm[l], mla_w_ukv[l])
        lambda_init = 0.8 - 0.6 * math.exp(-0.3 * l)
        y_diff = diff_mixer(dq, dk, dv, positions, diff_lambda_q1[l], diff_lambda_k1[l],
                            diff_lambda_q2[l], diff_lambda_k2[l], diff_subln[l], lambda_init)
        y_pool = pool_mixer(p_in, pool_w[l], pool_b[l], pool_scale[l])

        branches = jnp.stack([y_mla, y_diff, y_pool], axis=2)
        branches = jnp.einsum('bsnc,ncd->bsnd', branches, w_branch[l])
        gates = jax.nn.sigmoid(z_gate.reshape(b, s, N_BRANCH, d))
        merged = jnp.sum(gates * branches, axis=2)
        h = h + merged @ w_out[l]

        h = h + 0.5 * swiglu(rmsnorm(h, ffn2_norm[l]), ffn2_w_gate[l], ffn2_w_up[l], ffn2_w_down[l])
    return rmsnorm(h, final_norm)
```

```python
import functools
import math

import jax
import jax.numpy as jnp
from jax import lax
from jax.experimental import pallas as pl
from jax.experimental.pallas import tpu as pltpu

F32 = jnp.float32
BF16 = jnp.bfloat16

EPS = 1e-6
NEG_INF = -1e30
LOG2E = math.log2(math.e)

MLA_HEADS = 8
MLA_NOPE_DIM = 64
MLA_ROPE_DIM = 32
MLA_V_DIM = 64
MLA_Q_RANK = 256
MLA_KV_RANK = 128
ROPE_BASE = 10000.0
MLA_HEAD_PAD = 128

DIFF_HEADS = 8
DIFF_HEAD_DIM = 32
DIFF_V_DIM = 2 * DIFF_HEAD_DIM

POOL_WINDOWS = (2, 4, 8, 16)
POOL_GROUP = 128
POOL_HALO = 16
N_BRANCH = 3
BRANCH_WIDTH = 512

LANES = 128
VMEM_LIMIT_BYTES = 56 * 1024 * 1024

FFN_TM = 1024
FFN_TF = 256
PROJ_TM = 512
ATT_T = 512
MERGE_TM = 512
TAB_TM = 1024


def _cparams(sem):
    return pltpu.CompilerParams(dimension_semantics=sem, vmem_limit_bytes=VMEM_LIMIT_BYTES)


def _rms(x, g):
    return x * lax.rsqrt(jnp.mean(x * x, axis=-1, keepdims=True) + EPS) * g


def _dot(a, b):
    return jnp.dot(a, b, preferred_element_type=F32)


def _dot_nt(a, b):
    return lax.dot_general(a, b, (((1,), (1,)), ((), ())), preferred_element_type=F32)


def _ffn_kernel(h_ref, g_ref, wg_ref, wu_ref, wd_ref, *rest, n_f, final):
    if final:
        fg_ref, o_ref, xn_ref, acc_ref = rest
    else:
        o_ref, xn_ref, acc_ref = rest
    f = pl.program_id(1)

    @pl.when(f == 0)
    def _():
        xn_ref[...] = _rms(h_ref[...], g_ref[...]).astype(BF16)
        acc_ref[...] = jnp.zeros_like(acc_ref)

    xn = xn_ref[...]
    a = _dot(xn, wg_ref[...])
    b = _dot(xn, wu_ref[...])
    hid = (a * jax.nn.sigmoid(a)) * b
    acc_ref[...] += _dot(hid.astype(BF16), wd_ref[...])

    @pl.when(f == n_f - 1)
    def _():
        y = h_ref[...] + 0.5 * acc_ref[...]
        if final:
            y = _rms(y, fg_ref[...])
        o_ref[...] = y


def _ffn(h, g, wg, wu, wd, final_g=None):
    t, d = h.shape
    ff = wg.shape[1]
    tm, tf = min(FFN_TM, t), FFN_TF
    assert t % tm == 0 and ff % tf == 0
    n_f = ff // tf
    final = final_g is not None
    in_specs = [
        pl.BlockSpec((tm, d), lambda i, f: (i, 0)),
        pl.BlockSpec((1, d), lambda i, f: (0, 0)),
        pl.BlockSpec((d, tf), lambda i, f: (0, f)),
        pl.BlockSpec((d, tf), lambda i, f: (0, f)),
        pl.BlockSpec((tf, d), lambda i, f: (f, 0)),
    ]
    args = [h, g.reshape(1, d), wg, wu, wd]
    if final:
        in_specs.append(pl.BlockSpec((1, d), lambda i, f: (0, 0)))
        args.append(final_g.reshape(1, d))
    return pl.pallas_call(
        functools.partial(_ffn_kernel, n_f=n_f, final=final),
        grid=(t // tm, n_f),
        in_specs=in_specs,
        out_specs=pl.BlockSpec((tm, d), lambda i, f: (i, 0)),
        out_shape=jax.ShapeDtypeStruct((t, d), F32),
        scratch_shapes=[pltpu.VMEM((tm, d), BF16), pltpu.VMEM((tm, d), F32)],
        compiler_params=_cparams(("parallel", "arbitrary")),
        name="ffn",
    )(*args)


def _rope_table_kernel(posc_ref, posr_ref, fr_ref, fc_ref, cn_ref, sn_ref, ct_ref, st_ref):
    ang_n = posc_ref[0].astype(F32) * fr_ref[...]
    cn_ref[0] = jnp.cos(ang_n)
    sn_ref[0] = jnp.sin(ang_n)
    ang_t = fc_ref[...] * posr_ref[0].astype(F32)
    ct_ref[0] = jnp.cos(ang_t)
    st_ref[0] = jnp.sin(ang_t)


def _rope_tables(positions, inv_freq2):
    b, s = positions.shape
    r = inv_freq2.shape[0]
    tm = min(TAB_TM, s)
    assert s % tm == 0
    nat = jax.ShapeDtypeStruct((b, s, r), F32)
    tr = jax.ShapeDtypeStruct((b, r, s), F32)
    return pl.pallas_call(
        _rope_table_kernel,
        grid=(b, s // tm),
        in_specs=[
            pl.BlockSpec((1, tm, 1), lambda bi, i: (bi, i, 0)),
            pl.BlockSpec((1, 1, tm), lambda bi, i: (bi, 0, i)),
            pl.BlockSpec((1, r), lambda bi, i: (0, 0)),
            pl.BlockSpec((r, 1), lambda bi, i: (0, 0)),
        ],
        out_specs=[
            pl.BlockSpec((1, tm, r), lambda bi, i: (bi, i, 0)),
            pl.BlockSpec((1, tm, r), lambda bi, i: (bi, i, 0)),
            pl.BlockSpec((1, r, tm), lambda bi, i: (bi, 0, i)),
            pl.BlockSpec((1, r, tm), lambda bi, i: (bi, 0, i)),
        ],
        out_shape=[nat, nat, tr, tr],
        compiler_params=_cparams(("parallel", "parallel")),
        name="rope_tables",
    )(positions.reshape(b, s, 1), positions.reshape(b, 1, s), inv_freq2.reshape(1, r), inv_freq2.reshape(r, 1))


def _proj_kernel(h_ref, g_ref, wn_ref, wt_ref, qn_ref, kvn_ref, w1_ref, w2_ref, wkn_ref, e_ref, wv_ref,
                 cn_ref, sn_ref, ct_ref, st_ref,
                 qm_ref, km_ref, vm_ref, qd_ref, kd_ref, vd_ref, pin_ref, *, n_vt, tk):
    r = MLA_ROPE_DIM
    u = _rms(h_ref[0], g_ref[...]).astype(BF16)
    zn = _dot(u, wn_ref[...])
    zt = _dot_nt(wt_ref[...], u)

    cqn = _rms(zn[:, :MLA_Q_RANK], qn_ref[...]).astype(BF16)
    ckvn = _rms(zn[:, MLA_Q_RANK:MLA_Q_RANK + MLA_KV_RANK], kvn_ref[...]).astype(BF16)
    o = MLA_Q_RANK + MLA_KV_RANK
    kr, krp = zn[:, o:o + r], zn[:, o + r:o + 2 * r]

    qscale = (MLA_NOPE_DIM + MLA_ROPE_DIM) ** -0.5 * LOG2E
    q1 = _dot_nt(w1_ref[...], cqn)
    q2 = _dot_nt(w2_ref[...], cqn)
    ct, st = ct_ref[0], st_ref[0]
    for hd in range(MLA_HEADS):
        base = hd * MLA_HEAD_PAD
        qm_ref[0, base:base + MLA_NOPE_DIM, :] = (q1[base:base + MLA_NOPE_DIM] * qscale).astype(BF16)
        ro = base + MLA_NOPE_DIM
        roped = q1[ro:ro + r] * ct + q2[hd * r:(hd + 1) * r] * st
        qm_ref[0, ro:ro + r, :] = (roped * qscale).astype(BF16)
        qm_ref[0, ro + r:base + MLA_HEAD_PAD, :] = jnp.zeros((MLA_HEAD_PAD - MLA_NOPE_DIM - r, q1.shape[1]), BF16)

    kro = (kr * cn_ref[0] + krp * sn_ref[0]).astype(BF16)
    km_ref[0] = (_dot(ckvn, wkn_ref[...]) + _dot(kro, e_ref[...])).astype(BF16)

    vmt = _dot_nt(wv_ref[...], ckvn).astype(BF16)
    qdt = (zt[:DIFF_HEADS * 2 * DIFF_HEAD_DIM] * (DIFF_HEAD_DIM ** -0.5 * LOG2E)).astype(BF16)
    vdt = zt[DIFF_HEADS * 2 * DIFF_HEAD_DIM:].astype(BF16)
    for j in range(n_vt):
        vm_ref[0, j] = vmt[:, j * tk:(j + 1) * tk]
        vd_ref[0, j] = vdt[:, j * tk:(j + 1) * tk]
    qd_ref[0] = qdt
    kd_ref[0] = zn[:, 512:1024].astype(BF16)
    pin_ref[0] = zn[:, 1024:1536]


def _proj(h, g, wts, tabs, tk):
    b, s, d = h.shape
    tm = min(PROJ_TM, s)
    assert s % tm == 0 and tm % tk == 0
    n_vt = tm // tk
    cn, sn, ct, st = tabs
    names = ("wn", "wt", "qn", "kvn", "w1", "w2", "wkn", "e", "wv")
    consts = [wts[n] for n in names]

    def cspec(a):
        return pl.BlockSpec(a.shape, lambda bi, i, _n=a.ndim: (0,) * _n)

    hm, hd_ = MLA_HEADS * MLA_HEAD_PAD, DIFF_HEADS * 2 * DIFF_HEAD_DIM
    vw = MLA_HEADS * MLA_V_DIM
    out_shape = [
        jax.ShapeDtypeStruct((b, hm, s), BF16),
        jax.ShapeDtypeStruct((b, s, hm), BF16),
        jax.ShapeDtypeStruct((b, s // tk, vw, tk), BF16),
        jax.ShapeDtypeStruct((b, hd_, s), BF16),
        jax.ShapeDtypeStruct((b, s, hd_), BF16),
        jax.ShapeDtypeStruct((b, s // tk, hd_, tk), BF16),
        jax.ShapeDtypeStruct((b, s, 512), F32),
    ]
    out_specs = [
        pl.BlockSpec((1, hm, tm), lambda bi, i: (bi, 0, i)),
        pl.BlockSpec((1, tm, hm), lambda bi, i: (bi, i, 0)),
        pl.BlockSpec((1, n_vt, vw, tk), lambda bi, i: (bi, i, 0, 0)),
        pl.BlockSpec((1, hd_, tm), lambda bi, i: (bi, 0, i)),
        pl.BlockSpec((1, tm, hd_), lambda bi, i: (bi, i, 0)),
        pl.BlockSpec((1, n_vt, hd_, tk), lambda bi, i: (bi, i, 0, 0)),
        pl.BlockSpec((1, tm, 512), lambda bi, i: (bi, i, 0)),
    ]
    r = MLA_ROPE_DIM
    in_specs = ([pl.BlockSpec((1, tm, d), lambda bi, i: (bi, i, 0)), pl.BlockSpec((1, d), lambda bi, i: (0, 0))]
                + [cspec(a) for a in consts]
                + [pl.BlockSpec((1, tm, r), lambda bi, i: (bi, i, 0))] * 2
                + [pl.BlockSpec((1, r, tm), lambda bi, i: (bi, 0, i))] * 2)
    return pl.pallas_call(
        functools.partial(_proj_kernel, n_vt=n_vt, tk=tk),
        grid=(b, s // tm),
        in_specs=in_specs,
        out_specs=out_specs,
        out_shape=out_shape,
        compiler_params=_cparams(("parallel", "parallel")),
        name="mixer_proj",
    )(h, g.reshape(1, d), *consts, cn, sn, ct, st)


def _softmax_step(c, s, vt, m_ref, l_ref, acc_ref):
    m_old = m_ref[c]
    m_new = jnp.maximum(m_old, jnp.max(s, axis=0, keepdims=True))
    alpha = jnp.exp2(m_old - m_new)
    p = jnp.exp2(s - m_new)
    l_ref[c] = alpha * l_ref[c] + jnp.sum(p, axis=0, keepdims=True)
    acc_ref[c] = alpha * acc_ref[c] + _dot(vt, p.astype(BF16))
    m_ref[c] = m_new


def _causal_tile_mask(t):
    return lax.broadcasted_iota(jnp.int32, (t, t), 0) <= lax.broadcasted_iota(jnp.int32, (t, t), 1)


def _init_state(m_ref, l_ref, acc_ref):
    m_ref[...] = jnp.full(m_ref.shape, NEG_INF, F32)
    l_ref[...] = jnp.zeros_like(l_ref)
    acc_ref[...] = jnp.zeros_like(acc_ref)


def _mla_attn_kernel(q_ref, k_ref, v_ref, o_ref, m_ref, l_ref, acc_ref, *, t):
    qi = pl.program_id(2)
    _init_state(m_ref, l_ref, acc_ref)
    hp = MLA_HEAD_PAD

    def step(kj, masked):
        ks = pl.multiple_of(kj * t, t)
        vt = v_ref[0, kj]
        for c in range(2):
            s = _dot(k_ref[0, pl.ds(ks, t), c * hp:(c + 1) * hp], q_ref[0, c * hp:(c + 1) * hp, :])
            if masked:
                s = jnp.where(_causal_tile_mask(t), s, NEG_INF)
            _softmax_step(c, s, vt[c * MLA_V_DIM:(c + 1) * MLA_V_DIM], m_ref, l_ref, acc_ref)

    def body(kj, carry):
        step(kj, False)
        return carry

    lax.fori_loop(0, qi, body, 0)
    step(qi, True)

    o = jnp.concatenate([acc_ref[c] / l_ref[c] for c in range(2)], axis=0)
    o_ref[0] = o.T.astype(BF16)


def _mla_attention(qm, km, vm, t):
    b, _, s = qm.shape
    n_pair = MLA_HEADS // 2
    nq = s // t
    hp2 = 2 * MLA_HEAD_PAD
    return pl.pallas_call(
        functools.partial(_mla_attn_kernel, t=t),
        grid=(b, n_pair, nq),
        in_specs=[
            pl.BlockSpec((1, hp2, t), lambda bi, p, qi: (bi, p, qi)),
            pl.BlockSpec((1, s, hp2), lambda bi, p, qi: (bi, 0, p)),
            pl.BlockSpec((1, nq, 2 * MLA_V_DIM, t), lambda bi, p, qi: (bi, 0, p, 0)),
        ],
        out_specs=pl.BlockSpec((1, t, 2 * MLA_V_DIM), lambda bi, p, qi: (bi, qi, p)),
        out_shape=jax.ShapeDtypeStruct((b, s, MLA_HEADS * MLA_V_DIM), BF16),
        scratch_shapes=[pltpu.VMEM((2, 1, t), F32), pltpu.VMEM((2, 1, t), F32),
                        pltpu.VMEM((2, MLA_V_DIM, t), F32)],
        compiler_params=_cparams(("parallel", "parallel", "arbitrary")),
        name="mla_attention",
    )(qm, km, vm)


def _diff_attn_kernel(q_ref, k_ref, v_ref, pq_ref, pk_ref, sl_ref, lam_ref, sub_ref, o_ref,
                      w_ref, m_ref, l_ref, acc_ref, *, t, lambda_init):
    qi = pl.program_id(2)
    _init_state(m_ref, l_ref, acc_ref)
    dh = DIFF_HEAD_DIM
    n_rep = t // LANES

    qt = q_ref[0]
    row = lax.broadcasted_iota(jnp.int32, qt.shape, 0)
    for c in range(4):
        w_ref[c] = jnp.where((row >= c * dh) & (row < (c + 1) * dh), qt, jnp.zeros_like(qt))

    pq = pq_ref[0]

    def step(kj, masked):
        ks = pl.multiple_of(kj * t, t)
        kt = k_ref[0, pl.ds(ks, t), :]
        vt = v_ref[0, kj]
        pk = pk_ref[0, pl.ds(ks, t), :]
        dist = jnp.concatenate(
            [jnp.abs(pq[:, j * LANES:(j + 1) * LANES] - pk) for j in range(n_rep)], axis=1).astype(F32)
        for hh in range(2):
            bias = dist * jnp.concatenate([sl_ref[0, hh:hh + 1, :]] * n_rep, axis=1)
            for mp in range(2):
                c = hh * 2 + mp
                s = _dot(kt, w_ref[c]) + bias
                if masked:
                    s = jnp.where(_causal_tile_mask(t), s, NEG_INF)
                _softmax_step(c, s, vt[hh * DIFF_V_DIM:(hh + 1) * DIFF_V_DIM], m_ref, l_ref, acc_ref)

    def body(kj, carry):
        step(kj, False)
        return carry

    lax.fori_loop(0, qi, body, 0)
    step(qi, True)

    lam_v = lam_ref[...]
    lam = (jnp.exp(jnp.sum(lam_v[0:1] * lam_v[1:2], axis=1, keepdims=True))
           - jnp.exp(jnp.sum(lam_v[2:3] * lam_v[3:4], axis=1, keepdims=True)) + lambda_init)
    sub = jnp.concatenate([sub_ref[...]] * n_rep, axis=1)
    outs = []
    for hh in range(2):
        c0, c1 = 2 * hh, 2 * hh + 1
        o = acc_ref[c0] / l_ref[c0] - lam * (acc_ref[c1] / l_ref[c1])
        o = o * lax.rsqrt(jnp.mean(o * o, axis=0, keepdims=True) + EPS) * sub
        outs.append(o * (1.0 - lambda_init))
    o_ref[0] = jnp.concatenate(outs, axis=0).T.astype(BF16)


def _diff_attention(qd, kd, vd, positions, neg_slopes, lam_params, subln, lambda_init, t):
    b, _, s = qd.shape
    n_pair = DIFF_HEADS // 2
    nq = s // t
    pq = positions.reshape(b, 1, s)
    pk = jnp.broadcast_to(positions[:, :, None], (b, s, LANES))
    sl = jnp.broadcast_to(neg_slopes.reshape(n_pair, 2, 1), (n_pair, 2, LANES))
    sub = jnp.broadcast_to(subln[:, None], (DIFF_V_DIM, LANES))
    return pl.pallas_call(
        functools.partial(_diff_attn_kernel, t=t, lambda_init=lambda_init),
        grid=(b, n_pair, nq),
        in_specs=[
            pl.BlockSpec((1, LANES, t), lambda bi, p, qi: (bi, p, qi)),
            pl.BlockSpec((1, s, LANES), lambda bi, p, qi: (bi, 0, p)),
            pl.BlockSpec((1, nq, 2 * DIFF_V_DIM, t), lambda bi, p, qi: (bi, 0, p, 0)),
            pl.BlockSpec((1, 1, t), lambda bi, p, qi: (bi, 0, qi)),
            pl.BlockSpec((1, s, LANES), lambda bi, p, qi: (bi, 0, 0)),
            pl.BlockSpec((1, 2, LANES), lambda bi, p, qi: (p, 0, 0)),
            pl.BlockSpec((4, DIFF_HEAD_DIM), lambda bi, p, qi: (0, 0)),
            pl.BlockSpec((DIFF_V_DIM, LANES), lambda bi, p, qi: (0, 0)),
        ],
        out_specs=pl.BlockSpec((1, t, 2 * DIFF_V_DIM), lambda bi, p, qi: (bi, qi, p)),
        out_shape=jax.ShapeDtypeStruct((b, s, DIFF_HEADS * DIFF_V_DIM), BF16),
        scratch_shapes=[pltpu.VMEM((4, LANES, t), BF16), pltpu.VMEM((4, 1, t), F32),
                        pltpu.VMEM((4, 1, t), F32), pltpu.VMEM((4, DIFF_V_DIM, t), F32)],
        compiler_params=_cparams(("parallel", "parallel", "arbitrary")),
        name="diff_attention",
    )(qd, kd, vd, pq, pk, sl, lam_params, sub)


def _merge_kernel(h_ref, g_ref, ym_ref, yd_ref, pin_ref, halo_ref, wgate_ref, wb_ref, pw_ref, pb_ref, ps_ref,
                  wo_ref, o_ref, ext_ref, *, tm):
    i = pl.program_id(1)
    h = h_ref[0]
    d = h.shape[1]
    u = _rms(h, g_ref[...]).astype(BF16)

    x = pin_ref[0]
    ext_ref[0:POOL_HALO, :] = jnp.where(i == 0, jnp.zeros_like(halo_ref[0]), halo_ref[0])
    ext_ref[POOL_HALO:, :] = x
    tpos = i * tm + lax.broadcasted_iota(jnp.int32, (tm, POOL_GROUP), 0)
    yp = []
    for gi, w in enumerate(POOL_WINDOWS):
        ls = slice(gi * POOL_GROUP, (gi + 1) * POOL_GROUP)
        tot = ext_ref[POOL_HALO:POOL_HALO + tm, ls]
        for j in range(1, w):
            tot = tot + ext_ref[POOL_HALO - j:POOL_HALO - j + tm, ls]
        cnt = jnp.minimum(tpos + 1, w).astype(F32)
        pooled = tot / cnt - x[:, ls]
        yp.append((_dot(pooled.astype(BF16), pw_ref[gi]) + pb_ref[gi:gi + 1, :]) * ps_ref[:, ls])
    y_pool = jnp.concatenate(yp, axis=1).astype(BF16)

    merged = jnp.zeros((tm, d), F32)
    for bi, y in enumerate((ym_ref[0], yd_ref[0], y_pool)):
        gate = jax.nn.sigmoid(_dot(u, wgate_ref[:, bi * d:(bi + 1) * d]))
        merged = merged + gate * _dot(y, wb_ref[bi])
    o_ref[0] = h + _dot(merged.astype(BF16), wo_ref[...])


def _merge(h, g, y_mla, y_diff, pin, wts):
    b, s, d = h.shape
    tm = min(MERGE_TM, s)
    assert s % tm == 0 and tm % POOL_HALO == 0
    hb = tm // POOL_HALO
    consts = [wts[n] for n in ("wgate", "wb", "pw", "pb", "ps", "wo")]

    def cspec(a):
        return pl.BlockSpec(a.shape, lambda bi, i, _n=a.ndim: (0,) * _n)

    tile = lambda w: pl.BlockSpec((1, tm, w), lambda bi, i: (bi, i, 0))
    return pl.pallas_call(
        functools.partial(_merge_kernel, tm=tm),
        grid=(b, s // tm),
        in_specs=[tile(d), pl.BlockSpec((1, d), lambda bi, i: (0, 0)), tile(BRANCH_WIDTH), tile(BRANCH_WIDTH),
                  tile(512),
                  pl.BlockSpec((1, POOL_HALO, 512), lambda bi, i: (bi, jnp.maximum(i * hb - 1, 0), 0))]
                 + [cspec(a) for a in consts],
        out_specs=tile(d),
        out_shape=jax.ShapeDtypeStruct((b, s, d), F32),
        scratch_shapes=[pltpu.VMEM((tm + POOL_HALO, 512), F32)],
        compiler_params=_cparams(("parallel", "parallel")),
        name="gated_merge",
    )(h, g.reshape(1, d), y_mla, y_diff, pin, pin, *consts)


def _rot_half_cols(w):
    half = w.shape[-1] // 2
    return jnp.concatenate([-w[..., half:], w[..., :half]], axis=-1)


def _layer_weights(l, w_in, mla_q_norm, mla_w_uq, mla_kv_norm, mla_w_ukv, pool_w, pool_b, pool_scale,
                   w_branch, w_out):
    d = w_in.shape[1]
    win = w_in[l]
    o_kv = MLA_Q_RANK
    o_kr = o_kv + MLA_KV_RANK
    o_dq = o_kr + MLA_ROPE_DIM
    o_dk = o_dq + 512
    o_dv = o_dk + 512
    o_p = o_dv + 512
    o_g = o_p + 512
    w_kr = win[:, o_kr:o_dq]
    wn = jnp.concatenate([win[:, :o_dq], _rot_half_cols(w_kr), jnp.zeros((d, 64), F32),
                          win[:, o_dk:o_dv], win[:, o_p:o_g]], axis=1)
    wt = jnp.concatenate([win[:, o_dq:o_dk], win[:, o_dv:o_p]], axis=1).T

    wq = mla_w_uq[l].reshape(MLA_Q_RANK, MLA_HEADS, MLA_NOPE_DIM + MLA_ROPE_DIM)
    rope = wq[..., MLA_NOPE_DIM:]
    pad = jnp.zeros((MLA_Q_RANK, MLA_HEADS, MLA_HEAD_PAD - MLA_NOPE_DIM - MLA_ROPE_DIM), F32)
    w1 = jnp.concatenate([wq, pad], axis=-1).reshape(MLA_Q_RANK, MLA_HEADS * MLA_HEAD_PAD).T
    w2 = _rot_half_cols(rope).reshape(MLA_Q_RANK, MLA_HEADS * MLA_ROPE_DIM).T

    wkv = mla_w_ukv[l]
    lane = jnp.arange(wkv.shape[1]) % (MLA_NOPE_DIM + MLA_V_DIM)
    wkn = jnp.where(lane[None, :] < MLA_NOPE_DIM, wkv, 0.0)
    wv = wkv.reshape(MLA_KV_RANK, MLA_HEADS, MLA_NOPE_DIM + MLA_V_DIM)[..., MLA_NOPE_DIM:]
    wv = wv.reshape(MLA_KV_RANK, MLA_HEADS * MLA_V_DIM).T
    col = jnp.arange(MLA_HEADS * MLA_HEAD_PAD)
    e = ((col[None, :] % MLA_HEAD_PAD) == (MLA_NOPE_DIM + jnp.arange(MLA_ROPE_DIM)[:, None])).astype(BF16)

    bf = lambda a: a.astype(BF16)
    return {
        "wn": bf(wn), "wt": bf(wt), "qn": mla_q_norm[l].reshape(1, -1), "kvn": mla_kv_norm[l].reshape(1, -1),
        "w1": bf(w1), "w2": bf(w2), "wkn": bf(wkn), "e": e, "wv": bf(wv),
        "wgate": bf(win[:, o_g:]), "wb": bf(w_branch[l]), "pw": bf(pool_w[l]), "pb": pool_b[l],
        "ps": pool_scale[l].reshape(1, -1), "wo": bf(w_out[l]),
    }


def kernel(x, positions, ffn1_norm, ffn1_w_gate, ffn1_w_up, ffn1_w_down, mix_norm, w_in, mla_q_norm, mla_w_uq, mla_kv_norm, mla_w_ukv, diff_lambda_q1, diff_lambda_k1, diff_lambda_q2, diff_lambda_k2, diff_subln, pool_w, pool_b, pool_scale, w_branch, w_out, ffn2_norm, ffn2_w_gate, ffn2_w_up, ffn2_w_down, final_norm):
    b, s, d = x.shape
    depth = w_in.shape[0]
    t = min(ATT_T, s)
    assert s % t == 0

    half = MLA_ROPE_DIM // 2
    inv_freq = ROPE_BASE ** (-jnp.arange(half, dtype=F32) / half)
    tabs = _rope_tables(positions, jnp.concatenate([inv_freq, inv_freq]))
    slopes = jnp.exp2(-8.0 * jnp.arange(1, DIFF_HEADS + 1, dtype=F32) / DIFF_HEADS)
    neg_slopes = -slopes * LOG2E

    bf = lambda a: a.astype(BF16)
    h = x.reshape(b * s, d)
    for l in range(depth):
        h = _ffn(h, ffn1_norm[l], bf(ffn1_w_gate[l]), bf(ffn1_w_up[l]), bf(ffn1_w_down[l]))
        wts = _layer_weights(l, w_in, mla_q_norm, mla_w_uq, mla_kv_norm, mla_w_ukv, pool_w, pool_b, pool_scale,
                             w_branch, w_out)
        h3 = h.reshape(b, s, d)
        qm, km, vm, qd, kd, vd, pin = _proj(h3, mix_norm[l], wts, tabs, t)
        y_mla = _mla_attention(qm, km, vm, t)
        lambda_init = 0.8 - 0.6 * math.exp(-0.3 * l)
        lam_params = jnp.stack([diff_lambda_q1[l], diff_lambda_k1[l], diff_lambda_q2[l], diff_lambda_k2[l]])
        y_diff = _diff_attention(qd, kd, vd, positions, neg_slopes, lam_params, diff_subln[l], lambda_init, t)
        h = _merge(h3, mix_norm[l], y_mla, y_diff, pin, wts).reshape(b * s, d)
        last = l == depth - 1
        h = _ffn(h, ffn2_norm[l], bf(ffn2_w_gate[l]), bf(ffn2_w_up[l]), bf(ffn2_w_down[l]),
                 final_g=final_norm if last else None)
    return h.reshape(b, s, d)
```

```python
import functools
import math

import jax
import jax.numpy as jnp
from jax import lax
from jax.experimental import pallas as pl
from jax.experimental.pallas import tpu as pltpu

F32 = jnp.float32
BF16 = jnp.bfloat16

EPS = 1e-6
NEG_INF = -1e30
LOG2E = math.log2(math.e)

MLA_HEADS = 8
MLA_NOPE_DIM = 64
MLA_ROPE_DIM = 32
MLA_V_DIM = 64
MLA_Q_RANK = 256
MLA_KV_RANK = 128
ROPE_BASE = 10000.0
MLA_HEAD_PAD = 128

DIFF_HEADS = 8
DIFF_HEAD_DIM = 32
DIFF_V_DIM = 2 * DIFF_HEAD_DIM

POOL_WINDOWS = (2, 4, 8, 16)
POOL_GROUP = 128
POOL_HALO = 16
N_BRANCH = 3
BRANCH_WIDTH = 512

LANES = 128
VMEM_LIMIT_BYTES = 56 * 1024 * 1024

FFN_TM = 1024
FFN_TF = 256
PROJ_TM = 512
ATT_T = 512
MERGE_TM = 512
TAB_TM = 1024


def _cparams(sem):
    return pltpu.CompilerParams(dimension_semantics=sem, vmem_limit_bytes=VMEM_LIMIT_BYTES)


def _rms(x, g):
    return x * lax.rsqrt(jnp.mean(x * x, axis=-1, keepdims=True) + EPS) * g


def _dot(a, b):
    return jnp.dot(a, b, preferred_element_type=F32)


def _dot_nt(a, b):
    return lax.dot_general(a, b, (((1,), (1,)), ((), ())), preferred_element_type=F32)


def _ffn_kernel(h_ref, g_ref, wg_ref, wu_ref, wd_ref, *rest, n_f, final):
    if final:
        fg_ref, o_ref, xn_ref, acc_ref = rest
    else:
        o_ref, xn_ref, acc_ref = rest
    f = pl.program_id(1)

    @pl.when(f == 0)
    def _():
        xn_ref[...] = _rms(h_ref[...], g_ref[...]).astype(BF16)
        acc_ref[...] = jnp.zeros_like(acc_ref)

    xn = xn_ref[...]
    a = _dot(xn, wg_ref[...])
    b = _dot(xn, wu_ref[...])
    hid = (a * jax.nn.sigmoid(a)) * b
    acc_ref[...] += _dot(hid.astype(BF16), wd_ref[...])

    @pl.when(f == n_f - 1)
    def _():
        y = h_ref[...] + 0.5 * acc_ref[...]
        if final:
            y = _rms(y, fg_ref[...])
        o_ref[...] = y


def _ffn(h, g, wg, wu, wd, final_g=None):
    t, d = h.shape
    ff = wg.shape[1]
    tm, tf = min(FFN_TM, t), FFN_TF
    assert t % tm == 0 and ff % tf == 0
    n_f = ff // tf
    final = final_g is not None
    in_specs = [
        pl.BlockSpec((tm, d), lambda i, f: (i, 0)),
        pl.BlockSpec((1, d), lambda i, f: (0, 0)),
        pl.BlockSpec((d, tf), lambda i, f: (0, f)),
        pl.BlockSpec((d, tf), lambda i, f: (0, f)),
        pl.BlockSpec((tf, d), lambda i, f: (f, 0)),
    ]
    args = [h, g.reshape(1, d), wg, wu, wd]
    if final:
        in_specs.append(pl.BlockSpec((1, d), lambda i, f: (0, 0)))
        args.append(final_g.reshape(1, d))
    return pl.pallas_call(
        functools.partial(_ffn_kernel, n_f=n_f, final=final),
        grid=(t // tm, n_f),
        in_specs=in_specs,
        out_specs=pl.BlockSpec((tm, d), lambda i, f: (i, 0)),
        out_shape=jax.ShapeDtypeStruct((t, d), F32),
        scratch_shapes=[pltpu.VMEM((tm, d), BF16), pltpu.VMEM((tm, d), F32)],
        compiler_params=_cparams(("parallel", "arbitrary")),
        name="ffn",
    )(*args)


def _rope_table_kernel(posc_ref, posr_ref, fr_ref, fc_ref, cn_ref, sn_ref, ct_ref, st_ref):
    ang_n = posc_ref[0].astype(F32) * fr_ref[...]
    cn_ref[0] = jnp.cos(ang_n)
    sn_ref[0] = jnp.sin(ang_n)
    ang_t = fc_ref[...] * posr_ref[0].astype(F32)
    ct_ref[0] = jnp.cos(ang_t)
    st_ref[0] = jnp.sin(ang_t)


def _rope_tables(positions, inv_freq2):
    b, s = positions.shape
    r = inv_freq2.shape[0]
    tm = min(TAB_TM, s)
    assert s % tm == 0
    nat = jax.ShapeDtypeStruct((b, s, r), F32)
    tr = jax.ShapeDtypeStruct((b, r, s), F32)
    return pl.pallas_call(
        _rope_table_kernel,
        grid=(b, s // tm),
        in_specs=[
            pl.BlockSpec((1, tm, 1), lambda bi, i: (bi, i, 0)),
            pl.BlockSpec((1, 1, tm), lambda bi, i: (bi, 0, i)),
            pl.BlockSpec((1, r), lambda bi, i: (0, 0)),
            pl.BlockSpec((r, 1), lambda bi, i: (0, 0)),
        ],
        out_specs=[
            pl.BlockSpec((1, tm, r), lambda bi, i: (bi, i, 0)),
            pl.BlockSpec((1, tm, r), lambda bi, i: (bi, i, 0)),
            pl.BlockSpec((1, r, tm), lambda bi, i: (bi, 0, i)),
            pl.BlockSpec((1, r, tm), lambda bi, i: (bi, 0, i)),
        ],
        out_shape=[nat, nat, tr, tr],
        compiler_params=_cparams(("parallel", "parallel")),
        name="rope_tables",
    )(positions.reshape(b, s, 1), positions.reshape(b, 1, s), inv_freq2.reshape(1, r), inv_freq2.reshape(r, 1))


def _proj_kernel(h_ref, g_ref, wn_ref, wt_ref, qn_ref, kvn_ref, w1_ref, w2_ref, wkn_ref, e_ref, wv_ref,
                 cn_ref, sn_ref, ct_ref, st_ref,
                 qm_ref, km_ref, vm_ref, qd_ref, kd_ref, vd_ref, pin_ref, *, n_vt, tk):
    r = MLA_ROPE_DIM
    u = _rms(h_ref[0], g_ref[...]).astype(BF16)
    zn = _dot(u, wn_ref[...])
    zt = _dot_nt(wt_ref[...], u)

    cqn = _rms(zn[:, :MLA_Q_RANK], qn_ref[...]).astype(BF16)
    ckvn = _rms(zn[:, MLA_Q_RANK:MLA_Q_RANK + MLA_KV_RANK], kvn_ref[...]).astype(BF16)
    o = MLA_Q_RANK + MLA_KV_RANK
    kr, krp = zn[:, o:o + r], zn[:, o + r:o + 2 * r]

    qscale = (MLA_NOPE_DIM + MLA_ROPE_DIM) ** -0.5 * LOG2E
    q1 = _dot_nt(w1_ref[...], cqn)
    q2 = _dot_nt(w2_ref[...], cqn)
    ct, st = ct_ref[0], st_ref[0]
    for hd in range(MLA_HEADS):
        base = hd * MLA_HEAD_PAD
        qm_ref[0, base:base + MLA_NOPE_DIM, :] = (q1[base:base + MLA_NOPE_DIM] * qscale).astype(BF16)
        ro = base + MLA_NOPE_DIM
        roped = q1[ro:ro + r] * ct + q2[hd * r:(hd + 1) * r] * st
        qm_ref[0, ro:ro + r, :] = (roped * qscale).astype(BF16)
        qm_ref[0, ro + r:base + MLA_HEAD_PAD, :] = jnp.zeros((MLA_HEAD_PAD - MLA_NOPE_DIM - r, q1.shape[1]), BF16)

    kro = (kr * cn_ref[0] + krp * sn_ref[0]).astype(BF16)
    km_ref[0] = (_dot(ckvn, wkn_ref[...]) + _dot(kro, e_ref[...])).astype(BF16)

    vmt = _dot_nt(wv_ref[...], ckvn).astype(BF16)
    qdt = (zt[:DIFF_HEADS * 2 * DIFF_HEAD_DIM] * (DIFF_HEAD_DIM ** -0.5 * LOG2E)).astype(BF16)
    vdt = zt[DIFF_HEADS * 2 * DIFF_HEAD_DIM:].astype(BF16)
    for j in range(n_vt):
        vm_ref[0, j] = vmt[:, j * tk:(j + 1) * tk]
        vd_ref[0, j] = vdt[:, j * tk:(j + 1) * tk]
    qd_ref[0] = qdt
    kd_ref[0] = zn[:, 512:1024].astype(BF16)
    pin_ref[0] = zn[:, 1024:1536]


def _proj(h, g, wts, tabs, tk):
    b, s, d = h.shape
    tm = min(PROJ_TM, s)
    assert s % tm == 0 and tm % tk == 0
    n_vt = tm // tk
    cn, sn, ct, st = tabs
    names = ("wn", "wt", "qn", "kvn", "w1", "w2", "wkn", "e", "wv")
    consts = [wts[n] for n in names]

    def cspec(a):
        return pl.BlockSpec(a.shape, lambda bi, i, _n=a.ndim: (0,) * _n)

    hm, hd_ = MLA_HEADS * MLA_HEAD_PAD, DIFF_HEADS * 2 * DIFF_HEAD_DIM
    vw = MLA_HEADS * MLA_V_DIM
    out_shape = [
        jax.ShapeDtypeStruct((b, hm, s), BF16),
        jax.ShapeDtypeStruct((b, s, hm), BF16),
        jax.ShapeDtypeStruct((b, s // tk, vw, tk), BF16),
        jax.ShapeDtypeStruct((b, hd_, s), BF16),
        jax.ShapeDtypeStruct((b, s, hd_), BF16),
        jax.ShapeDtypeStruct((b, s // tk, hd_, tk), BF16),
        jax.ShapeDtypeStruct((b, s, 512), F32),
    ]
    out_specs = [
        pl.BlockSpec((1, hm, tm), lambda bi, i: (bi, 0, i)),
        pl.BlockSpec((1, tm, hm), lambda bi, i: (bi, i, 0)),
        pl.BlockSpec((1, n_vt, vw, tk), lambda bi, i: (bi, i, 0, 0)),
        pl.BlockSpec((1, hd_, tm), lambda bi, i: (bi, 0, i)),
        pl.BlockSpec((1, tm, hd_), lambda bi, i: (bi, i, 0)),
        pl.BlockSpec((1, n_vt, hd_, tk), lambda bi, i: (bi, i, 0, 0)),
        pl.BlockSpec((1, tm, 512), lambda bi, i: (bi, i, 0)),
    ]
    r = MLA_ROPE_DIM
    in_specs = ([pl.BlockSpec((1, tm, d), lambda bi, i: (bi, i, 0)), pl.BlockSpec((1, d), lambda bi, i: (0, 0))]
                + [cspec(a) for a in consts]
                + [pl.BlockSpec((1, tm, r), lambda bi, i: (bi, i, 0))] * 2
                + [pl.BlockSpec((1, r, tm), lambda bi, i: (bi, 0, i))] * 2)
    return pl.pallas_call(
        functools.partial(_proj_kernel, n_vt=n_vt, tk=tk),
        grid=(b, s // tm),
        in_specs=in_specs,
        out_specs=out_specs,
        out_shape=out_shape,
        compiler_params=_cparams(("parallel", "parallel")),
        name="mixer_proj",
    )(h, g.reshape(1, d), *consts, cn, sn, ct, st)


ONES_ROWS = 16


def _causal_tile_mask(t):
    return lax.broadcasted_iota(jnp.int32, (t, t), 0) <= lax.broadcasted_iota(jnp.int32, (t, t), 1)


def _flash_causal(qi, n_chains, t, prep_fn, score_fn, value_fn, s_ref, mt_ref, m_ref, l_ref, acc_ref):
    dv = acc_ref.shape[1]
    ones = jnp.ones((ONES_ROWS, t), BF16)

    m_ref[...] = jnp.full(m_ref.shape, NEG_INF, F32)
    l_ref[...] = jnp.zeros_like(l_ref)
    acc_ref[...] = jnp.zeros_like(acc_ref)

    def put(c, s):
        s_ref[c] = s
        mt_ref[c] = jnp.max(s, axis=0, keepdims=True)

    def consume(kj, c):
        m_old = m_ref[c]
        m_new = jnp.maximum(m_old, mt_ref[c])
        alpha = jnp.exp2(m_old - m_new)
        p = jnp.exp2(s_ref[c] - m_new).astype(BF16)
        r = _dot(jnp.concatenate([value_fn(kj, c), ones], axis=0), p)
        acc_ref[c] = alpha * acc_ref[c] + r[:dv]
        l_ref[c] = alpha * l_ref[c] + r[dv:dv + 1]
        m_ref[c] = m_new

    def fill(kj, masked):
        ctx = prep_fn(kj)
        for c in range(n_chains):
            put(c, score_fn(ctx, c, masked))

    def step(kj, masked_next):
        ctx = prep_fn(kj + 1)
        for c in range(n_chains):
            s_next = score_fn(ctx, c, masked_next)
            consume(kj, c)
            put(c, s_next)

    @pl.when(qi == 0)
    def _():
        fill(0, True)

    @pl.when(qi > 0)
    def _():
        fill(0, False)

    def body(kj, carry):
        step(kj, False)
        return carry

    lax.fori_loop(0, qi - 1, body, 0)

    @pl.when(qi > 0)
    def _():
        step(qi - 1, True)

    for c in range(n_chains):
        consume(qi, c)


def _mla_attn_kernel(q_ref, k_ref, v_ref, o_ref, s_ref, mt_ref, m_ref, l_ref, acc_ref, *, t):
    qi = pl.program_id(2)
    hp = MLA_HEAD_PAD

    def scores(ks, c, masked):
        s = _dot(k_ref[0, pl.ds(ks, t), c * hp:(c + 1) * hp], q_ref[0, c * hp:(c + 1) * hp, :])
        return jnp.where(_causal_tile_mask(t), s, NEG_INF) if masked else s

    _flash_causal(qi, 2, t, lambda kj: pl.multiple_of(kj * t, t), scores,
                  lambda kj, c: v_ref[0, kj, c * MLA_V_DIM:(c + 1) * MLA_V_DIM, :],
                  s_ref, mt_ref, m_ref, l_ref, acc_ref)

    o = jnp.concatenate([acc_ref[c] / l_ref[c] for c in range(2)], axis=0)
    o_ref[0] = o.T.astype(BF16)


def _mla_attention(qm, km, vm, t):
    b, _, s = qm.shape
    n_pair = MLA_HEADS // 2
    nq = s // t
    hp2 = 2 * MLA_HEAD_PAD
    return pl.pallas_call(
        functools.partial(_mla_attn_kernel, t=t),
        grid=(b, n_pair, nq),
        in_specs=[
            pl.BlockSpec((1, hp2, t), lambda bi, p, qi: (bi, p, qi)),
            pl.BlockSpec((1, s, hp2), lambda bi, p, qi: (bi, 0, p)),
            pl.BlockSpec((1, nq, 2 * MLA_V_DIM, t), lambda bi, p, qi: (bi, 0, p, 0)),
        ],
        out_specs=pl.BlockSpec((1, t, 2 * MLA_V_DIM), lambda bi, p, qi: (bi, qi, p)),
        out_shape=jax.ShapeDtypeStruct((b, s, MLA_HEADS * MLA_V_DIM), BF16),
        scratch_shapes=[pltpu.VMEM((2, t, t), F32), pltpu.VMEM((2, 1, t), F32), pltpu.VMEM((2, 1, t), F32),
                        pltpu.VMEM((2, 1, t), F32), pltpu.VMEM((2, MLA_V_DIM, t), F32)],
        compiler_params=_cparams(("parallel", "parallel", "arbitrary")),
        name="mla_attention",
    )(qm, km, vm)


def _diff_attn_kernel(q_ref, k_ref, v_ref, pq_ref, pk_ref, sl_ref, lam_ref, sub_ref, o_ref,
                      w_ref, s_ref, mt_ref, m_ref, l_ref, acc_ref, *, t, lambda_init):
    qi = pl.program_id(2)
    dh = DIFF_HEAD_DIM
    n_rep = t // LANES

    qt = q_ref[0]
    row = lax.broadcasted_iota(jnp.int32, qt.shape, 0)
    for c in range(4):
        w_ref[c] = jnp.where((row >= c * dh) & (row < (c + 1) * dh), qt, jnp.zeros_like(qt))

    pq = pq_ref[0]

    def prep(kj):
        ks = pl.multiple_of(kj * t, t)
        pk = pk_ref[0, pl.ds(ks, t), :]
        dist = jnp.concatenate(
            [jnp.abs(pq[:, j * LANES:(j + 1) * LANES] - pk) for j in range(n_rep)], axis=1).astype(F32)
        bias = [dist * jnp.concatenate([sl_ref[0, hh:hh + 1, :]] * n_rep, axis=1) for hh in range(2)]
        return ks, bias

    def scores(ctx, c, masked):
        ks, bias = ctx
        s = _dot(k_ref[0, pl.ds(ks, t), :], w_ref[c]) + bias[c // 2]
        return jnp.where(_causal_tile_mask(t), s, NEG_INF) if masked else s

    _flash_causal(qi, 4, t, prep, scores,
                  lambda kj, c: v_ref[0, kj, (c // 2) * DIFF_V_DIM:(c // 2 + 1) * DIFF_V_DIM, :],
                  s_ref, mt_ref, m_ref, l_ref, acc_ref)

    lam_v = lam_ref[...]
    lam = (jnp.exp(jnp.sum(lam_v[0:1] * lam_v[1:2], axis=1, keepdims=True))
           - jnp.exp(jnp.sum(lam_v[2:3] * lam_v[3:4], axis=1, keepdims=True)) + lambda_init)
    sub = jnp.concatenate([sub_ref[...]] * n_rep, axis=1)
    outs = []
    for hh in range(2):
        c0, c1 = 2 * hh, 2 * hh + 1
        o = acc_ref[c0] / l_ref[c0] - lam * (acc_ref[c1] / l_ref[c1])
        o = o * lax.rsqrt(jnp.mean(o * o, axis=0, keepdims=True) + EPS) * sub
        outs.append(o * (1.0 - lambda_init))
    o_ref[0] = jnp.concatenate(outs, axis=0).T.astype(BF16)


def _diff_attention(qd, kd, vd, positions, neg_slopes, lam_params, subln, lambda_init, t):
    b, _, s = qd.shape
    n_pair = DIFF_HEADS // 2
    nq = s // t
    pq = positions.reshape(b, 1, s)
    pk = jnp.broadcast_to(positions[:, :, None], (b, s, LANES))
    sl = jnp.broadcast_to(neg_slopes.reshape(n_pair, 2, 1), (n_pair, 2, LANES))
    sub = jnp.broadcast_to(subln[:, None], (DIFF_V_DIM, LANES))
    return pl.pallas_call(
        functools.partial(_diff_attn_kernel, t=t, lambda_init=lambda_init),
        grid=(b, n_pair, nq),
        in_specs=[
            pl.BlockSpec((1, LANES, t), lambda bi, p, qi: (bi, p, qi)),
            pl.BlockSpec((1, s, LANES), lambda bi, p, qi: (bi, 0, p)),
            pl.BlockSpec((1, nq, 2 * DIFF_V_DIM, t), lambda bi, p, qi: (bi, 0, p, 0)),
            pl.BlockSpec((1, 1, t), lambda bi, p, qi: (bi, 0, qi)),
            pl.BlockSpec((1, s, LANES), lambda bi, p, qi: (bi, 0, 0)),
            pl.BlockSpec((1, 2, LANES), lambda bi, p, qi: (p, 0, 0)),
            pl.BlockSpec((4, DIFF_HEAD_DIM), lambda bi, p, qi: (0, 0)),
            pl.BlockSpec((DIFF_V_DIM, LANES), lambda bi, p, qi: (0, 0)),
        ],
        out_specs=pl.BlockSpec((1, t, 2 * DIFF_V_DIM), lambda bi, p, qi: (bi, qi, p)),
        out_shape=jax.ShapeDtypeStruct((b, s, DIFF_HEADS * DIFF_V_DIM), BF16),
        scratch_shapes=[pltpu.VMEM((4, LANES, t), BF16), pltpu.VMEM((4, t, t), F32), pltpu.VMEM((4, 1, t), F32),
                        pltpu.VMEM((4, 1, t), F32), pltpu.VMEM((4, 1, t), F32),
                        pltpu.VMEM((4, DIFF_V_DIM, t), F32)],
        compiler_params=_cparams(("parallel", "parallel", "arbitrary")),
        name="diff_attention",
    )(qd, kd, vd, pq, pk, sl, lam_params, sub)


def _merge_kernel(h_ref, g_ref, ym_ref, yd_ref, pin_ref, halo_ref, wgate_ref, wb_ref, pw_ref, pb_ref, ps_ref,
                  wo_ref, o_ref, ext_ref, *, tm):
    i = pl.program_id(1)
    h = h_ref[0]
    d = h.shape[1]
    u = _rms(h, g_ref[...]).astype(BF16)

    x = pin_ref[0]
    ext_ref[0:POOL_HALO, :] = jnp.where(i == 0, jnp.zeros_like(halo_ref[0]), halo_ref[0])
    ext_ref[POOL_HALO:, :] = x
    tpos = i * tm + lax.broadcasted_iota(jnp.int32, (tm, POOL_GROUP), 0)
    yp = []
    for gi, w in enumerate(POOL_WINDOWS):
        ls = slice(gi * POOL_GROUP, (gi + 1) * POOL_GROUP)
        tot = ext_ref[POOL_HALO:POOL_HALO + tm, ls]
        for j in range(1, w):
            tot = tot + ext_ref[POOL_HALO - j:POOL_HALO - j + tm, ls]
        cnt = jnp.minimum(tpos + 1, w).astype(F32)
        pooled = tot / cnt - x[:, ls]
        yp.append((_dot(pooled.astype(BF16), pw_ref[gi]) + pb_ref[gi:gi + 1, :]) * ps_ref[:, ls])
    y_pool = jnp.concatenate(yp, axis=1).astype(BF16)

    merged = jnp.zeros((tm, d), F32)
    for bi, y in enumerate((ym_ref[0], yd_ref[0], y_pool)):
        gate = jax.nn.sigmoid(_dot(u, wgate_ref[:, bi * d:(bi + 1) * d]))
        merged = merged + gate * _dot(y, wb_ref[bi])
    o_ref[0] = h + _dot(merged.astype(BF16), wo_ref[...])


def _merge(h, g, y_mla, y_diff, pin, wts):
    b, s, d = h.shape
    tm = min(MERGE_TM, s)
    assert s % tm == 0 and tm % POOL_HALO == 0
    hb = tm // POOL_HALO
    consts = [wts[n] for n in ("wgate", "wb", "pw", "pb", "ps", "wo")]

    def cspec(a):
        return pl.BlockSpec(a.shape, lambda bi, i, _n=a.ndim: (0,) * _n)

    tile = lambda w: pl.BlockSpec((1, tm, w), lambda bi, i: (bi, i, 0))
    return pl.pallas_call(
        functools.partial(_merge_kernel, tm=tm),
        grid=(b, s // tm),
        in_specs=[tile(d), pl.BlockSpec((1, d), lambda bi, i: (0, 0)), tile(BRANCH_WIDTH), tile(BRANCH_WIDTH),
                  tile(512),
                  pl.BlockSpec((1, POOL_HALO, 512), lambda bi, i: (bi, jnp.maximum(i * hb - 1, 0), 0))]
                 + [cspec(a) for a in consts],
        out_specs=tile(d),
        out_shape=jax.ShapeDtypeStruct((b, s, d), F32),
        scratch_shapes=[pltpu.VMEM((tm + POOL_HALO, 512), F32)],
        compiler_params=_cparams(("parallel", "parallel")),
        name="gated_merge",
    )(h, g.reshape(1, d), y_mla, y_diff, pin, pin, *consts)


def _rot_half_cols(w):
    half = w.shape[-1] // 2
    return jnp.concatenate([-w[..., half:], w[..., :half]], axis=-1)


def _layer_weights(l, w_in, mla_q_norm, mla_w_uq, mla_kv_norm, mla_w_ukv, pool_w, pool_b, pool_scale,
                   w_branch, w_out):
    d = w_in.shape[1]
    win = w_in[l]
    o_kv = MLA_Q_RANK
    o_kr = o_kv + MLA_KV_RANK
    o_dq = o_kr + MLA_ROPE_DIM
    o_dk = o_dq + 512
    o_dv = o_dk + 512
    o_p = o_dv + 512
    o_g = o_p + 512
    w_kr = win[:, o_kr:o_dq]
    wn = jnp.concatenate([win[:, :o_dq], _rot_half_cols(w_kr), jnp.zeros((d, 64), F32),
                          win[:, o_dk:o_dv], win[:, o_p:o_g]], axis=1)
    wt = jnp.concatenate([win[:, o_dq:o_dk], win[:, o_dv:o_p]], axis=1).T

    wq = mla_w_uq[l].reshape(MLA_Q_RANK, MLA_HEADS, MLA_NOPE_DIM + MLA_ROPE_DIM)
    rope = wq[..., MLA_NOPE_DIM:]
    pad = jnp.zeros((MLA_Q_RANK, MLA_HEADS, MLA_HEAD_PAD - MLA_NOPE_DIM - MLA_ROPE_DIM), F32)
    w1 = jnp.concatenate([wq, pad], axis=-1).reshape(MLA_Q_RANK, MLA_HEADS * MLA_HEAD_PAD).T
    w2 = _rot_half_cols(rope).reshape(MLA_Q_RANK, MLA_HEADS * MLA_ROPE_DIM).T

    wkv = mla_w_ukv[l]
    lane = jnp.arange(wkv.shape[1]) % (MLA_NOPE_DIM + MLA_V_DIM)
    wkn = jnp.where(lane[None, :] < MLA_NOPE_DIM, wkv, 0.0)
    wv = wkv.reshape(MLA_KV_RANK, MLA_HEADS, MLA_NOPE_DIM + MLA_V_DIM)[..., MLA_NOPE_DIM:]
    wv = wv.reshape(MLA_KV_RANK, MLA_HEADS * MLA_V_DIM).T
    col = jnp.arange(MLA_HEADS * MLA_HEAD_PAD)
    e = ((col[None, :] % MLA_HEAD_PAD) == (MLA_NOPE_DIM + jnp.arange(MLA_ROPE_DIM)[:, None])).astype(BF16)

    bf = lambda a: a.astype(BF16)
    return {
        "wn": bf(wn), "wt": bf(wt), "qn": mla_q_norm[l].reshape(1, -1), "kvn": mla_kv_norm[l].reshape(1, -1),
        "w1": bf(w1), "w2": bf(w2), "wkn": bf(wkn), "e": e, "wv": bf(wv),
        "wgate": bf(win[:, o_g:]), "wb": bf(w_branch[l]), "pw": bf(pool_w[l]), "pb": pool_b[l],
        "ps": pool_scale[l].reshape(1, -1), "wo": bf(w_out[l]),
    }


def kernel(x, positions, ffn1_norm, ffn1_w_gate, ffn1_w_up, ffn1_w_down, mix_norm, w_in, mla_q_norm, mla_w_uq, mla_kv_norm, mla_w_ukv, diff_lambda_q1, diff_lambda_k1, diff_lambda_q2, diff_lambda_k2, diff_subln, pool_w, pool_b, pool_scale, w_branch, w_out, ffn2_norm, ffn2_w_gate, ffn2_w_up, ffn2_w_down, final_norm):
    b, s, d = x.shape
    depth = w_in.shape[0]
    t = min(ATT_T, s)
    assert s % t == 0

    half = MLA_ROPE_DIM // 2
    inv_freq = ROPE_BASE ** (-jnp.arange(half, dtype=F32) / half)
    tabs = _rope_tables(positions, jnp.concatenate([inv_freq, inv_freq]))
    slopes = jnp.exp2(-8.0 * jnp.arange(1, DIFF_HEADS + 1, dtype=F32) / DIFF_HEADS)
    neg_slopes = -slopes * LOG2E

    bf = lambda a: a.astype(BF16)
    h = x.reshape(b * s, d)
    for l in range(depth):
        h = _ffn(h, ffn1_norm[l], bf(ffn1_w_gate[l]), bf(ffn1_w_up[l]), bf(ffn1_w_down[l]))
        wts = _layer_weights(l, w_in, mla_q_norm, mla_w_uq, mla_kv_norm, mla_w_ukv, pool_w, pool_b, pool_scale,
                             w_branch, w_out)
        h3 = h.reshape(b, s, d)
        qm, km, vm, qd, kd, vd, pin = _proj(h3, mix_norm[l], wts, tabs, t)
        y_mla = _mla_attention(qm, km, vm, t)
        lambda_init = 0.8 - 0.6 * math.exp(-0.3 * l)
        lam_params = jnp.stack([diff_lambda_q1[l], diff_lambda_k1[l], diff_lambda_q2[l], diff_lambda_k2[l]])
        y_diff = _diff_attention(qd, kd, vd, positions, neg_slopes, lam_params, diff_subln[l], lambda_init, t)
        h = _merge(h3, mix_norm[l], y_mla, y_diff, pin, wts).reshape(b * s, d)
        last = l == depth - 1
        h = _ffn(h, ffn2_norm[l], bf(ffn2_w_gate[l]), bf(ffn2_w_up[l]), bf(ffn2_w_down[l]),
                 final_g=final_norm if last else None)
    return h.reshape(b, s, d)
```

```python
import functools
import math

import jax
import jax.numpy as jnp
from jax import lax
from jax.experimental import pallas as pl
from jax.experimental.pallas import tpu as pltpu

F32 = jnp.float32
BF16 = jnp.bfloat16

EPS = 1e-6
NEG_INF = -1e30
LOG2E = math.log2(math.e)

MLA_HEADS = 8
MLA_NOPE_DIM = 64
MLA_ROPE_DIM = 32
MLA_V_DIM = 64
MLA_Q_RANK = 256
MLA_KV_RANK = 128
ROPE_BASE = 10000.0
MLA_HEAD_PAD = 128

DIFF_HEADS = 8
DIFF_HEAD_DIM = 32
DIFF_V_DIM = 2 * DIFF_HEAD_DIM

POOL_WINDOWS = (2, 4, 8, 16)
POOL_GROUP = 128
POOL_HALO = 16
N_BRANCH = 3
BRANCH_WIDTH = 512

LANES = 128
VMEM_LIMIT_BYTES = 56 * 1024 * 1024

FFN_TM = 1024
FFN_TF = 256
ATT_TQ = 512
ATT_TK = 512
MERGE_TM = 512
TAB_TM = 1024


def _cparams(sem):
    return pltpu.CompilerParams(dimension_semantics=sem, vmem_limit_bytes=VMEM_LIMIT_BYTES)


def _rms(x, g):
    return x * lax.rsqrt(jnp.mean(x * x, axis=-1, keepdims=True) + EPS) * g


def _dot(a, b):
    return jnp.dot(a, b, preferred_element_type=F32)


def _dot_nt(a, b):
    return lax.dot_general(a, b, (((1,), (1,)), ((), ())), preferred_element_type=F32)


def _ffn_kernel(h_ref, g_ref, wg_ref, wu_ref, wd_ref, *rest, n_f, final):
    if final:
        fg_ref, o_ref, xn_ref, acc_ref = rest
    else:
        o_ref, xn_ref, acc_ref = rest
    f = pl.program_id(1)

    @pl.when(f == 0)
    def _():
        xn_ref[...] = _rms(h_ref[...], g_ref[...]).astype(BF16)
        acc_ref[...] = jnp.zeros_like(acc_ref)

    xn = xn_ref[...]
    a = _dot(xn, wg_ref[...])
    b = _dot(xn, wu_ref[...])
    hid = (a * jax.nn.sigmoid(a)) * b
    acc_ref[...] += _dot(hid.astype(BF16), wd_ref[...])

    @pl.when(f == n_f - 1)
    def _():
        y = h_ref[...] + 0.5 * acc_ref[...]
        if final:
            y = _rms(y, fg_ref[...])
        o_ref[...] = y


def _ffn(h, g, wg, wu, wd, l, final_g=None):
    t, d = h.shape
    ff = wg.shape[2]
    tm, tf = min(FFN_TM, t), FFN_TF
    assert t % tm == 0 and ff % tf == 0
    n_f = ff // tf
    final = final_g is not None
    in_specs = [
        pl.BlockSpec((tm, d), lambda i, f: (i, 0)),
        pl.BlockSpec((1, d), lambda i, f: (0, 0)),
        pl.BlockSpec((None, d, tf), lambda i, f: (l, 0, f)),
        pl.BlockSpec((None, d, tf), lambda i, f: (l, 0, f)),
        pl.BlockSpec((None, tf, d), lambda i, f: (l, f, 0)),
    ]
    args = [h, g.reshape(1, d), wg, wu, wd]
    if final:
        in_specs.append(pl.BlockSpec((1, d), lambda i, f: (0, 0)))
        args.append(final_g.reshape(1, d))
    return pl.pallas_call(
        functools.partial(_ffn_kernel, n_f=n_f, final=final),
        grid=(t // tm, n_f),
        in_specs=in_specs,
        out_specs=pl.BlockSpec((tm, d), lambda i, f: (i, 0)),
        out_shape=jax.ShapeDtypeStruct((t, d), F32),
        scratch_shapes=[pltpu.VMEM((tm, d), BF16), pltpu.VMEM((tm, d), F32)],
        compiler_params=_cparams(("parallel", "arbitrary")),
        name="ffn",
    )(*args)


def _rope_table_kernel(posc_ref, posr_ref, fr_ref, fc_ref, cn_ref, sn_ref, ct_ref, st_ref):
    ang_n = posc_ref[0].astype(F32) * fr_ref[...]
    cn_ref[0] = jnp.cos(ang_n)
    sn_ref[0] = jnp.sin(ang_n)
    ang_t = fc_ref[...] * posr_ref[0].astype(F32)
    ct_ref[0] = jnp.cos(ang_t)
    st_ref[0] = jnp.sin(ang_t)


def _rope_tables(positions, inv_freq2):
    b, s = positions.shape
    r = inv_freq2.shape[0]
    tm = min(TAB_TM, s)
    assert s % tm == 0
    nat = jax.ShapeDtypeStruct((b, s, r), F32)
    tr = jax.ShapeDtypeStruct((b, r, s), F32)
    return pl.pallas_call(
        _rope_table_kernel,
        grid=(b, s // tm),
        in_specs=[
            pl.BlockSpec((1, tm, 1), lambda bi, i: (bi, i, 0)),
            pl.BlockSpec((1, 1, tm), lambda bi, i: (bi, 0, i)),
            pl.BlockSpec((1, r), lambda bi, i: (0, 0)),
            pl.BlockSpec((r, 1), lambda bi, i: (0, 0)),
        ],
        out_specs=[
            pl.BlockSpec((1, tm, r), lambda bi, i: (bi, i, 0)),
            pl.BlockSpec((1, tm, r), lambda bi, i: (bi, i, 0)),
            pl.BlockSpec((1, r, tm), lambda bi, i: (bi, 0, i)),
            pl.BlockSpec((1, r, tm), lambda bi, i: (bi, 0, i)),
        ],
        out_shape=[nat, nat, tr, tr],
        compiler_params=_cparams(("parallel", "parallel")),
        name="rope_tables",
    )(positions.reshape(b, s, 1), positions.reshape(b, 1, s), inv_freq2.reshape(1, r), inv_freq2.reshape(r, 1))


def _proj_kernel(h_ref, g_ref, wn_ref, wt_ref, qn_ref, kvn_ref, w1_ref, w2_ref, wkn_ref, e_ref, wv_ref,
                 cn_ref, sn_ref, ct_ref, st_ref,
                 qm_ref, km_ref, vm_ref, qd_ref, kd_ref, vd_ref, pin_ref):
    r = MLA_ROPE_DIM
    u = _rms(h_ref[0], g_ref[...]).astype(BF16)
    zn = _dot(u, wn_ref[...])
    zt = _dot_nt(wt_ref[...], u)

    cqn = _rms(zn[:, :MLA_Q_RANK], qn_ref[...]).astype(BF16)
    ckvn = _rms(zn[:, MLA_Q_RANK:MLA_Q_RANK + MLA_KV_RANK], kvn_ref[...]).astype(BF16)
    o = MLA_Q_RANK + MLA_KV_RANK
    kr, krp = zn[:, o:o + r], zn[:, o + r:o + 2 * r]

    qscale = (MLA_NOPE_DIM + MLA_ROPE_DIM) ** -0.5 * LOG2E
    q1 = _dot_nt(w1_ref[...], cqn)
    q2 = _dot_nt(w2_ref[...], cqn)
    ct, st = ct_ref[0], st_ref[0]
    for hd in range(MLA_HEADS):
        base = hd * MLA_HEAD_PAD
        qm_ref[0, 0, base:base + MLA_NOPE_DIM, :] = (q1[base:base + MLA_NOPE_DIM] * qscale).astype(BF16)
        ro = base + MLA_NOPE_DIM
        roped = q1[ro:ro + r] * ct + q2[hd * r:(hd + 1) * r] * st
        qm_ref[0, 0, ro:ro + r, :] = (roped * qscale).astype(BF16)
        qm_ref[0, 0, ro + r:base + MLA_HEAD_PAD, :] = jnp.zeros((MLA_HEAD_PAD - MLA_NOPE_DIM - r, q1.shape[1]), BF16)

    kro = (kr * cn_ref[0] + krp * sn_ref[0]).astype(BF16)
    km_ref[0] = (_dot(ckvn, wkn_ref[...]) + _dot(kro, e_ref[...])).astype(BF16)

    vmt = _dot_nt(wv_ref[...], ckvn).astype(BF16)
    qdt = (zt[:DIFF_HEADS * 2 * DIFF_HEAD_DIM] * (DIFF_HEAD_DIM ** -0.5 * LOG2E)).astype(BF16)
    vdt = zt[DIFF_HEADS * 2 * DIFF_HEAD_DIM:].astype(BF16)
    vm_ref[0, 0] = vmt
    vd_ref[0, 0] = vdt
    qd_ref[0, 0] = qdt
    kd_ref[0] = zn[:, 512:1024].astype(BF16)
    pin_ref[0] = zn[:, 1024:1536]


def _proj(h, g, wts, tabs, tq, tk):
    b, s, d = h.shape
    tm = tk
    assert s % tq == 0 and tq % tm == 0
    qr = tq // tm
    cn, sn, ct, st = tabs
    names = ("wn", "wt", "qn", "kvn", "w1", "w2", "wkn", "e", "wv")
    consts = [wts[n] for n in names]

    def cspec(a):
        return pl.BlockSpec(a.shape, lambda bi, i, _n=a.ndim: (0,) * _n)

    hm, hd_ = MLA_HEADS * MLA_HEAD_PAD, DIFF_HEADS * 2 * DIFF_HEAD_DIM
    vw = MLA_HEADS * MLA_V_DIM
    out_shape = [
        jax.ShapeDtypeStruct((b, s // tq, hm, tq), BF16),
        jax.ShapeDtypeStruct((b, s, hm), BF16),
        jax.ShapeDtypeStruct((b, s // tk, vw, tk), BF16),
        jax.ShapeDtypeStruct((b, s // tq, hd_, tq), BF16),
        jax.ShapeDtypeStruct((b, s, hd_), BF16),
        jax.ShapeDtypeStruct((b, s // tk, hd_, tk), BF16),
        jax.ShapeDtypeStruct((b, s, 512), F32),
    ]
    qmaj = lambda rows: pl.BlockSpec((1, 1, rows, tm), lambda bi, i: (bi, i // qr, 0, i % qr))
    vmaj = lambda rows: pl.BlockSpec((1, 1, rows, tk), lambda bi, i: (bi, i, 0, 0))
    out_specs = [
        qmaj(hm),
        pl.BlockSpec((1, tm, hm), lambda bi, i: (bi, i, 0)),
        vmaj(vw),
        qmaj(hd_),
        pl.BlockSpec((1, tm, hd_), lambda bi, i: (bi, i, 0)),
        vmaj(hd_),
        pl.BlockSpec((1, tm, 512), lambda bi, i: (bi, i, 0)),
    ]
    r = MLA_ROPE_DIM
    in_specs = ([pl.BlockSpec((1, tm, d), lambda bi, i: (bi, i, 0)), pl.BlockSpec((1, d), lambda bi, i: (0, 0))]
                + [cspec(a) for a in consts]
                + [pl.BlockSpec((1, tm, r), lambda bi, i: (bi, i, 0))] * 2
                + [pl.BlockSpec((1, r, tm), lambda bi, i: (bi, 0, i))] * 2)
    return pl.pallas_call(
        _proj_kernel,
        grid=(b, s // tm),
        in_specs=in_specs,
        out_specs=out_specs,
        out_shape=out_shape,
        compiler_params=_cparams(("parallel", "parallel")),
        name="mixer_proj",
    )(h, g.reshape(1, d), *consts, cn, sn, ct, st)


ONES_ROWS = 16


def _tile_rows(i, t):
    return pl.ds(i * t, t) if isinstance(i, int) else pl.ds(pl.multiple_of(i * t, t), t)


def _flash_causal(nq, n_chains, tq, tk, switch_fn, prep_fn, score_fn, value_fn, finish_fn,
                  s_ref, mt_ref, m_ref, l_ref, acc_ref):
    dv = acc_ref.shape[1]
    r = tq // tk
    ones = jnp.ones((ONES_ROWS, tk), BF16)

    def masked_scores(ctx, c, j):
        row = lax.broadcasted_iota(jnp.int32, (tk, tq), 0)
        col = lax.broadcasted_iota(jnp.int32, (tk, tq), 1)
        return jnp.where(row + j * tk <= col, score_fn(ctx, c), NEG_INF)

    l_ref[...] = jnp.zeros_like(l_ref)
    acc_ref[...] = jnp.zeros_like(acc_ref)

    def put(c, s):
        s_ref[c] = s
        mt_ref[c] = jnp.max(s, axis=0, keepdims=True)

    def consume(kj, c):
        m_old = jnp.where(kj == 0, NEG_INF, m_ref[c])
        m_new = jnp.maximum(m_old, mt_ref[c])
        alpha = jnp.exp2(m_old - m_new)
        p = jnp.exp2(s_ref[c] - m_new).astype(BF16)
        r = _dot(jnp.concatenate([value_fn(kj, c), ones], axis=0), p)
        acc_ref[c] = alpha * acc_ref[c] + r[:dv]
        l_ref[c] = alpha * l_ref[c] + r[dv:dv + 1]
        m_ref[c] = m_new

    def step(kj, qn, kn, diag_next):
        ctx = prep_fn(qn, kn)
        for c in range(n_chains):
            s_next = score_fn(ctx, c) if diag_next is None else masked_scores(ctx, c, diag_next)
            consume(kj, c)
            put(c, s_next)

    if switch_fn is not None:
        switch_fn(0)
    ctx0 = prep_fn(0, 0)
    for c in range(n_chains):
        put(c, masked_scores(ctx0, c, 0))

    def body(_, carry):
        qi, kj = carry
        last = kj == r * qi + (r - 1)
        qn = jnp.where(last, qi + 1, qi)
        kn = jnp.where(last, 0, kj + 1)

        if switch_fn is not None:
            pl.when(last)(lambda: switch_fn(qn))

        for j in range(r):
            pl.when(kn == r * qn + j)(functools.partial(step, kj, qn, kn, j))
        pl.when(kn < r * qn)(functools.partial(step, kj, qn, kn, None))

        @pl.when(last)
        def _():
            finish_fn(qi)

        return qn, kn

    n_tiles = r * nq * (nq + 1) // 2
    lax.fori_loop(0, n_tiles - 1, body, (jnp.int32(0), jnp.int32(0)))
    for c in range(n_chains):
        consume(jnp.int32(r * nq - 1), c)
    finish_fn(nq - 1)


def _mla_attn_kernel(q_ref, k_ref, v_ref, o_ref, s_ref, mt_ref, m_ref, l_ref, acc_ref, *, tq, tk, nq):
    hp = MLA_HEAD_PAD

    def prep(qn, kn):
        return qn, pl.multiple_of(kn * tk, tk)

    def scores(ctx, c):
        qn, ks = ctx
        return _dot(k_ref[0, pl.ds(ks, tk), c * hp:(c + 1) * hp], q_ref[0, qn, c * hp:(c + 1) * hp, :])

    def finish(qi):
        o = jnp.concatenate([acc_ref[c] / l_ref[c] for c in range(2)], axis=0)
        o_ref[0, _tile_rows(qi, tq), :] = o.T.astype(BF16)

    _flash_causal(nq, 2, tq, tk, None, prep, scores,
                  lambda kj, c: v_ref[0, kj, c * MLA_V_DIM:(c + 1) * MLA_V_DIM, :], finish,
                  s_ref, mt_ref, m_ref, l_ref, acc_ref)


def _mla_attention(qm, km, vm):
    b, nq, _, tq = qm.shape
    nk, tk = vm.shape[1], vm.shape[3]
    s = nq * tq
    n_pair = MLA_HEADS // 2
    hp2 = 2 * MLA_HEAD_PAD
    return pl.pallas_call(
        functools.partial(_mla_attn_kernel, tq=tq, tk=tk, nq=nq),
        grid=(b, n_pair),
        in_specs=[
            pl.BlockSpec((1, nq, hp2, tq), lambda bi, p: (bi, 0, p, 0)),
            pl.BlockSpec((1, s, hp2), lambda bi, p: (bi, 0, p)),
            pl.BlockSpec((1, nk, 2 * MLA_V_DIM, tk), lambda bi, p: (bi, 0, p, 0)),
        ],
        out_specs=pl.BlockSpec((1, s, 2 * MLA_V_DIM), lambda bi, p: (bi, 0, p)),
        out_shape=jax.ShapeDtypeStruct((b, s, MLA_HEADS * MLA_V_DIM), BF16),
        scratch_shapes=[pltpu.VMEM((2, tk, tq), F32), pltpu.VMEM((2, 1, tq), F32), pltpu.VMEM((2, 1, tq), F32),
                        pltpu.VMEM((2, 1, tq), F32), pltpu.VMEM((2, MLA_V_DIM, tq), F32)],
        compiler_params=_cparams(("parallel", "parallel")),
        name="mla_attention",
    )(qm, km, vm)


ALIBI_ROWS = 16
ALIBI_SPLIT = 3
POS_DIGIT_BITS = 8


def _bf16_pieces(x):
    pieces = []
    for _ in range(ALIBI_SPLIT):
        p = x.astype(BF16).astype(F32)
        pieces.append(p)
        x = x - p
    return pieces


def _diff_attn_kernel(q_ref, k_ref, v_ref, pq_ref, pk_ref, c_ref, lam_ref, sub_ref, o_ref,
                      w_ref, s_ref, mt_ref, m_ref, l_ref, acc_ref, *, tq, tk, nq, lambda_init, sorted_pos):
    dh = DIFF_HEAD_DIM
    n_rep = tq // LANES
    lane_tile = lambda v: jnp.concatenate([v] * n_rep, axis=1)

    if sorted_pos:
        w_ref[:, LANES + ALIBI_ROWS:, :] = jnp.zeros((4, LANES - ALIBI_ROWS, tq), BF16)

    def switch(qn):
        qt = q_ref[0, qn]
        row = lax.broadcasted_iota(jnp.int32, qt.shape, 0)
        for c in range(4):
            w_ref[c, 0:LANES, :] = jnp.where((row >= c * dh) & (row < (c + 1) * dh), qt, jnp.zeros_like(qt))
        if sorted_pos:
            relq = pq_ref[0, qn].astype(F32)
            for hh in range(2):
                cp = [lane_tile(c_ref[0, ALIBI_SPLIT * hh + i:ALIBI_SPLIT * hh + i + 1, :])
                      for i in range(ALIBI_SPLIT)]
                tp = _bf16_pieces(-(cp[0] + cp[1] + cp[2]) * relq)
                rows = ([float(1 << POS_DIGIT_BITS) * p for p in cp] + cp + tp
                        + [jnp.zeros((ALIBI_ROWS - 3 * ALIBI_SPLIT, tq), F32)])
                blk = jnp.concatenate(rows, axis=0).astype(BF16)
                for mp in range(2):
                    w_ref[2 * hh + mp, LANES:LANES + ALIBI_ROWS, :] = blk

    def prep(qn, kn):
        ks = pl.multiple_of(kn * tk, tk)
        if sorted_pos:
            return ks, None
        pq = pq_ref[0, qn]
        pk = pk_ref[0, pl.ds(ks, tk), :]
        dist = jnp.concatenate(
            [jnp.abs(pq[:, j * LANES:(j + 1) * LANES] - pk) for j in range(n_rep)], axis=1).astype(F32)
        return ks, [dist * lane_tile(c_ref[0, hh:hh + 1, :]) for hh in range(2)]

    def scores(ctx, c):
        ks, bias = ctx
        if sorted_pos:
            lhs = jnp.concatenate([k_ref[0, pl.ds(ks, tk), :], pk_ref[0, pl.ds(ks, tk), :]], axis=1)
            return _dot(lhs, w_ref[c])
        return _dot(k_ref[0, pl.ds(ks, tk), :], w_ref[c]) + bias[c // 2]

    def finish(qi):
        lam_v = lam_ref[...]
        lam = (jnp.exp(jnp.sum(lam_v[0:1] * lam_v[1:2], axis=1, keepdims=True))
               - jnp.exp(jnp.sum(lam_v[2:3] * lam_v[3:4], axis=1, keepdims=True)) + lambda_init)
        sub = lane_tile(sub_ref[...])
        outs = []
        for hh in range(2):
            c0, c1 = 2 * hh, 2 * hh + 1
            o = acc_ref[c0] / l_ref[c0] - lam * (acc_ref[c1] / l_ref[c1])
            o = o * lax.rsqrt(jnp.mean(o * o, axis=0, keepdims=True) + EPS) * sub
            outs.append(o * (1.0 - lambda_init))
        o_ref[0, _tile_rows(qi, tq), :] = jnp.concatenate(outs, axis=0).T.astype(BF16)

    _flash_causal(nq, 4, tq, tk, switch, prep, scores,
                  lambda kj, c: v_ref[0, kj, (c // 2) * DIFF_V_DIM:(c // 2 + 1) * DIFF_V_DIM, :], finish,
                  s_ref, mt_ref, m_ref, l_ref, acc_ref)


def _diff_attention_call(qd, kd, vd, pq, pk, cvec, lam_params, sub, lambda_init, sorted_pos):
    b, nq, _, tq = qd.shape
    nk, tk = vd.shape[1], vd.shape[3]
    s = nq * tq
    t = tq
    n_pair = DIFF_HEADS // 2
    w_rows = 2 * LANES if sorted_pos else LANES
    return pl.pallas_call(
        functools.partial(_diff_attn_kernel, tq=tq, tk=tk, nq=nq, lambda_init=lambda_init, sorted_pos=sorted_pos),
        grid=(b, n_pair),
        in_specs=[
            pl.BlockSpec((1, nq, LANES, tq), lambda bi, p: (bi, 0, p, 0)),
            pl.BlockSpec((1, s, LANES), lambda bi, p: (bi, 0, p)),
            pl.BlockSpec((1, nk, 2 * DIFF_V_DIM, tk), lambda bi, p: (bi, 0, p, 0)),
            pl.BlockSpec((1, nq, 1, tq), lambda bi, p: (bi, 0, 0, 0)),
            pl.BlockSpec((1, s, LANES), lambda bi, p: (bi, 0, 0)),
            pl.BlockSpec((1,) + cvec.shape[1:], lambda bi, p: (p, 0, 0)),
            pl.BlockSpec((4, DIFF_HEAD_DIM), lambda bi, p: (0, 0)),
            pl.BlockSpec((DIFF_V_DIM, LANES), lambda bi, p: (0, 0)),
        ],
        out_specs=pl.BlockSpec((1, s, 2 * DIFF_V_DIM), lambda bi, p: (bi, 0, p)),
        out_shape=jax.ShapeDtypeStruct((b, s, DIFF_HEADS * DIFF_V_DIM), BF16),
        scratch_shapes=[pltpu.VMEM((4, w_rows, t), BF16), pltpu.VMEM((4, tk, t), F32), pltpu.VMEM((4, 1, t), F32),
                        pltpu.VMEM((4, 1, t), F32), pltpu.VMEM((4, 1, t), F32),
                        pltpu.VMEM((4, DIFF_V_DIM, t), F32)],
        compiler_params=_cparams(("parallel", "parallel")),
        name="diff_attention_sorted" if sorted_pos else "diff_attention",
    )(qd, kd, vd, pq, pk, cvec, lam_params, sub)


def _diff_attention(qd, kd, vd, positions, slopes, lam_params, subln, lambda_init):
    b, nq, _, t = qd.shape
    s = nq * t
    n_pair = DIFF_HEADS // 2
    c = (slopes * LOG2E).reshape(n_pair, 2, 1)
    sub = jnp.broadcast_to(subln[:, None], (DIFF_V_DIM, LANES))
    rel = positions - positions[:, :1]
    span = 1 << (2 * POS_DIGIT_BITS)
    sorted_pos = (jnp.all(positions[:, 1:] >= positions[:, :-1]) & jnp.all((rel >= 0) & (rel < span)))

    def sorted_path(_):
        hi = (rel >> POS_DIGIT_BITS).astype(BF16)
        lo = (rel & ((1 << POS_DIGIT_BITS) - 1)).astype(BF16)
        one = jnp.ones_like(hi)
        feats = jnp.stack([hi] * ALIBI_SPLIT + [lo] * ALIBI_SPLIT + [one] * ALIBI_SPLIT, axis=-1)
        feats = jnp.pad(feats, ((0, 0), (0, 0), (0, LANES - feats.shape[-1])))
        pieces = jnp.concatenate(_bf16_pieces(c), axis=-1).reshape(n_pair, 2 * ALIBI_SPLIT, 1)
        cvec = jnp.broadcast_to(jnp.pad(pieces, ((0, 0), (0, 8 - 2 * ALIBI_SPLIT), (0, 0))), (n_pair, 8, LANES))
        return _diff_attention_call(qd, kd, vd, rel.reshape(b, nq, 1, t), feats, cvec, lam_params, sub,
                                    lambda_init, True)

    def general_path(_):
        pk = jnp.broadcast_to(positions[:, :, None], (b, s, LANES))
        cvec = jnp.broadcast_to(-c, (n_pair, 2, LANES))
        return _diff_attention_call(qd, kd, vd, positions.reshape(b, nq, 1, t), pk, cvec, lam_params, sub,
                                    lambda_init, False)

    return lax.cond(sorted_pos, sorted_path, general_path, None)


def _merge_kernel(h_ref, g_ref, ym_ref, yd_ref, pin_ref, halo_ref, wgate_ref, wb_ref, pw_ref, pb_ref, ps_ref,
                  wo_ref, o_ref, ext_ref, *, tm):
    i = pl.program_id(1)
    h = h_ref[0]
    d = h.shape[1]
    u = _rms(h, g_ref[...]).astype(BF16)

    x = pin_ref[0]
    ext_ref[0:POOL_HALO, :] = jnp.where(i == 0, jnp.zeros_like(halo_ref[0]), halo_ref[0])
    ext_ref[POOL_HALO:, :] = x
    tpos = i * tm + lax.broadcasted_iota(jnp.int32, (tm, POOL_GROUP), 0)
    yp = []
    for gi, w in enumerate(POOL_WINDOWS):
        ls = slice(gi * POOL_GROUP, (gi + 1) * POOL_GROUP)
        tot = ext_ref[POOL_HALO:POOL_HALO + tm, ls]
        for j in range(1, w):
            tot = tot + ext_ref[POOL_HALO - j:POOL_HALO - j + tm, ls]
        cnt = jnp.minimum(tpos + 1, w).astype(F32)
        pooled = tot / cnt - x[:, ls]
        yp.append((_dot(pooled.astype(BF16), pw_ref[gi]) + pb_ref[gi:gi + 1, :]) * ps_ref[:, ls])
    y_pool = jnp.concatenate(yp, axis=1).astype(BF16)

    merged = jnp.zeros((tm, d), F32)
    for bi, y in enumerate((ym_ref[0], yd_ref[0], y_pool)):
        gate = jax.nn.sigmoid(_dot(u, wgate_ref[:, bi * d:(bi + 1) * d]))
        merged = merged + gate * _dot(y, wb_ref[bi])
    o_ref[0] = h + _dot(merged.astype(BF16), wo_ref[...])


def _merge(h, g, y_mla, y_diff, pin, wts):
    b, s, d = h.shape
    tm = min(MERGE_TM, s)
    assert s % tm == 0 and tm % POOL_HALO == 0
    hb = tm // POOL_HALO
    consts = [wts[n] for n in ("wgate", "wb", "pw", "pb", "ps", "wo")]

    def cspec(a):
        return pl.BlockSpec(a.shape, lambda bi, i, _n=a.ndim: (0,) * _n)

    tile = lambda w: pl.BlockSpec((1, tm, w), lambda bi, i: (bi, i, 0))
    return pl.pallas_call(
        functools.partial(_merge_kernel, tm=tm),
        grid=(b, s // tm),
        in_specs=[tile(d), pl.BlockSpec((1, d), lambda bi, i: (0, 0)), tile(BRANCH_WIDTH), tile(BRANCH_WIDTH),
                  tile(512),
                  pl.BlockSpec((1, POOL_HALO, 512), lambda bi, i: (bi, jnp.maximum(i * hb - 1, 0), 0))]
                 + [cspec(a) for a in consts],
        out_specs=tile(d),
        out_shape=jax.ShapeDtypeStruct((b, s, d), F32),
        scratch_shapes=[pltpu.VMEM((tm + POOL_HALO, 512), F32)],
        compiler_params=_cparams(("parallel", "parallel")),
        name="gated_merge",
    )(h, g.reshape(1, d), y_mla, y_diff, pin, pin, *consts)


def _rot_half_cols(w):
    half = w.shape[-1] // 2
    return jnp.concatenate([-w[..., half:], w[..., :half]], axis=-1)


def _layer_weights(l, w_in, mla_q_norm, mla_w_uq, mla_kv_norm, mla_w_ukv, pool_w, pool_b, pool_scale,
                   w_branch, w_out):
    d = w_in.shape[1]
    win = w_in[l]
    o_kv = MLA_Q_RANK
    o_kr = o_kv + MLA_KV_RANK
    o_dq = o_kr + MLA_ROPE_DIM
    o_dk = o_dq + 512
    o_dv = o_dk + 512
    o_p = o_dv + 512
    o_g = o_p + 512
    w_kr = win[:, o_kr:o_dq]
    wn = jnp.concatenate([win[:, :o_dq], _rot_half_cols(w_kr), jnp.zeros((d, 64), F32),
                          win[:, o_dk:o_dv], win[:, o_p:o_g]], axis=1)
    wt = jnp.concatenate([win[:, o_dq:o_dk], win[:, o_dv:o_p]], axis=1).T

    wq = mla_w_uq[l].reshape(MLA_Q_RANK, MLA_HEADS, MLA_NOPE_DIM + MLA_ROPE_DIM)
    rope = wq[..., MLA_NOPE_DIM:]
    pad = jnp.zeros((MLA_Q_RANK, MLA_HEADS, MLA_HEAD_PAD - MLA_NOPE_DIM - MLA_ROPE_DIM), F32)
    w1 = jnp.concatenate([wq, pad], axis=-1).reshape(MLA_Q_RANK, MLA_HEADS * MLA_HEAD_PAD).T
    w2 = _rot_half_cols(rope).reshape(MLA_Q_RANK, MLA_HEADS * MLA_ROPE_DIM).T

    wkv = mla_w_ukv[l]
    lane = jnp.arange(wkv.shape[1]) % (MLA_NOPE_DIM + MLA_V_DIM)
    wkn = jnp.where(lane[None, :] < MLA_NOPE_DIM, wkv, 0.0)
    wv = wkv.reshape(MLA_KV_RANK, MLA_HEADS, MLA_NOPE_DIM + MLA_V_DIM)[..., MLA_NOPE_DIM:]
    wv = wv.reshape(MLA_KV_RANK, MLA_HEADS * MLA_V_DIM).T
    col = jnp.arange(MLA_HEADS * MLA_HEAD_PAD)
    e = ((col[None, :] % MLA_HEAD_PAD) == (MLA_NOPE_DIM + jnp.arange(MLA_ROPE_DIM)[:, None])).astype(BF16)

    bf = lambda a: a.astype(BF16)
    return {
        "wn": bf(wn), "wt": bf(wt), "qn": mla_q_norm[l].reshape(1, -1), "kvn": mla_kv_norm[l].reshape(1, -1),
        "w1": bf(w1), "w2": bf(w2), "wkn": bf(wkn), "e": e, "wv": bf(wv),
        "wgate": bf(win[:, o_g:]), "wb": bf(w_branch[l]), "pw": bf(pool_w[l]), "pb": pool_b[l],
        "ps": pool_scale[l].reshape(1, -1), "wo": bf(w_out[l]),
    }


def kernel(x, positions, ffn1_norm, ffn1_w_gate, ffn1_w_up, ffn1_w_down, mix_norm, w_in, mla_q_norm, mla_w_uq, mla_kv_norm, mla_w_ukv, diff_lambda_q1, diff_lambda_k1, diff_lambda_q2, diff_lambda_k2, diff_subln, pool_w, pool_b, pool_scale, w_branch, w_out, ffn2_norm, ffn2_w_gate, ffn2_w_up, ffn2_w_down, final_norm):
    b, s, d = x.shape
    depth = w_in.shape[0]
    tq, tk = min(ATT_TQ, s), min(ATT_TK, s)
    assert s % tq == 0

    half = MLA_ROPE_DIM // 2
    inv_freq = ROPE_BASE ** (-jnp.arange(half, dtype=F32) / half)
    tabs = _rope_tables(positions, jnp.concatenate([inv_freq, inv_freq]))
    slopes = jnp.exp2(-8.0 * jnp.arange(1, DIFF_HEADS + 1, dtype=F32) / DIFF_HEADS)

    bf = lambda a: a.astype(BF16)
    ffn1_w = (bf(ffn1_w_gate), bf(ffn1_w_up), bf(ffn1_w_down))
    ffn2_w = (bf(ffn2_w_gate), bf(ffn2_w_up), bf(ffn2_w_down))
    h = x.reshape(b * s, d)
    for l in range(depth):
        h = _ffn(h, ffn1_norm[l], *ffn1_w, l)
        wts = _layer_weights(l, w_in, mla_q_norm, mla_w_uq, mla_kv_norm, mla_w_ukv, pool_w, pool_b, pool_scale,
                             w_branch, w_out)
        h3 = h.reshape(b, s, d)
        qm, km, vm, qd, kd, vd, pin = _proj(h3, mix_norm[l], wts, tabs, tq, tk)
        y_mla = _mla_attention(qm, km, vm)
        lambda_init = 0.8 - 0.6 * math.exp(-0.3 * l)
        lam_params = jnp.stack([diff_lambda_q1[l], diff_lambda_k1[l], diff_lambda_q2[l], diff_lambda_k2[l]])
        y_diff = _diff_attention(qd, kd, vd, positions, slopes, lam_params, diff_subln[l], lambda_init)
        h = _merge(h3, mix_norm[l], y_mla, y_diff, pin, wts).reshape(b * s, d)
        last = l == depth - 1
        h = _ffn(h, ffn2_norm[l], *ffn2_w, l, final_g=final_norm if last else None)
    return h.reshape(b, s, d)
```

```python
import functools
import math

import jax
import jax.numpy as jnp
from jax import lax
from jax.experimental import pallas as pl
from jax.experimental.pallas import tpu as pltpu

F32 = jnp.float32
BF16 = jnp.bfloat16

EPS = 1e-6
NEG_INF = -1e30
LOG2E = math.log2(math.e)

MLA_HEADS = 8
MLA_NOPE_DIM = 64
MLA_ROPE_DIM = 32
MLA_V_DIM = 64
MLA_Q_RANK = 256
MLA_KV_RANK = 128
ROPE_BASE = 10000.0
MLA_HEAD_PAD = 128

DIFF_HEADS = 8
DIFF_HEAD_DIM = 32
DIFF_V_DIM = 2 * DIFF_HEAD_DIM

POOL_WINDOWS = (2, 4, 8, 16)
POOL_GROUP = 128
POOL_HALO = 16
N_BRANCH = 3
BRANCH_WIDTH = 512

LANES = 128
VMEM_LIMIT_BYTES = 56 * 1024 * 1024

FFN_TM = 1024
FFN_TF = 256
ATT_TQ = 512
ATT_TK = 512
ATT_COL_SPLIT = 1
MERGE_TM = 512
TAB_TM = 1024


def _cparams(sem):
    return pltpu.CompilerParams(dimension_semantics=sem, vmem_limit_bytes=VMEM_LIMIT_BYTES)


def _rms(x, g):
    return x * lax.rsqrt(jnp.mean(x * x, axis=-1, keepdims=True) + EPS) * g


def _dot(a, b):
    return jnp.dot(a, b, preferred_element_type=F32)


def _dot_nt(a, b):
    return lax.dot_general(a, b, (((1,), (1,)), ((), ())), preferred_element_type=F32)


def _ffn_kernel(h_ref, g_ref, wg_ref, wu_ref, wd_ref, *rest, n_f, final):
    if final:
        fg_ref, o_ref, xn_ref, acc_ref = rest
    else:
        o_ref, xn_ref, acc_ref = rest
    f = pl.program_id(1)

    @pl.when(f == 0)
    def _():
        xn_ref[...] = _rms(h_ref[...], g_ref[...]).astype(BF16)
        acc_ref[...] = jnp.zeros_like(acc_ref)

    xn = xn_ref[...]
    a = _dot(xn, wg_ref[...])
    b = _dot(xn, wu_ref[...])
    hid = (a * jax.nn.sigmoid(a)) * b
    acc_ref[...] += _dot(hid.astype(BF16), wd_ref[...])

    @pl.when(f == n_f - 1)
    def _():
        y = h_ref[...] + 0.5 * acc_ref[...]
        if final:
            y = _rms(y, fg_ref[...])
        o_ref[...] = y


def _ffn(h, g, wg, wu, wd, l, final_g=None):
    t, d = h.shape
    ff = wg.shape[2]
    tm, tf = min(FFN_TM, t), FFN_TF
    assert t % tm == 0 and ff % tf == 0
    n_f = ff // tf
    final = final_g is not None
    in_specs = [
        pl.BlockSpec((tm, d), lambda i, f: (i, 0)),
        pl.BlockSpec((1, d), lambda i, f: (0, 0)),
        pl.BlockSpec((None, d, tf), lambda i, f: (l, 0, f)),
        pl.BlockSpec((None, d, tf), lambda i, f: (l, 0, f)),
        pl.BlockSpec((None, tf, d), lambda i, f: (l, f, 0)),
    ]
    args = [h, g.reshape(1, d), wg, wu, wd]
    if final:
        in_specs.append(pl.BlockSpec((1, d), lambda i, f: (0, 0)))
        args.append(final_g.reshape(1, d))
    return pl.pallas_call(
        functools.partial(_ffn_kernel, n_f=n_f, final=final),
        grid=(t // tm, n_f),
        in_specs=in_specs,
        out_specs=pl.BlockSpec((tm, d), lambda i, f: (i, 0)),
        out_shape=jax.ShapeDtypeStruct((t, d), F32),
        scratch_shapes=[pltpu.VMEM((tm, d), BF16), pltpu.VMEM((tm, d), F32)],
        compiler_params=_cparams(("parallel", "arbitrary")),
        name="ffn",
    )(*args)


def _rope_table_kernel(posc_ref, posr_ref, fr_ref, fc_ref, cn_ref, sn_ref, ct_ref, st_ref):
    ang_n = posc_ref[0].astype(F32) * fr_ref[...]
    cn_ref[0] = jnp.cos(ang_n)
    sn_ref[0] = jnp.sin(ang_n)
    ang_t = fc_ref[...] * posr_ref[0].astype(F32)
    ct_ref[0] = jnp.cos(ang_t)
    st_ref[0] = jnp.sin(ang_t)


def _rope_tables(positions, inv_freq2):
    b, s = positions.shape
    r = inv_freq2.shape[0]
    tm = min(TAB_TM, s)
    assert s % tm == 0
    nat = jax.ShapeDtypeStruct((b, s, r), F32)
    tr = jax.ShapeDtypeStruct((b, r, s), F32)
    return pl.pallas_call(
        _rope_table_kernel,
        grid=(b, s // tm),
        in_specs=[
            pl.BlockSpec((1, tm, 1), lambda bi, i: (bi, i, 0)),
            pl.BlockSpec((1, 1, tm), lambda bi, i: (bi, 0, i)),
            pl.BlockSpec((1, r), lambda bi, i: (0, 0)),
            pl.BlockSpec((r, 1), lambda bi, i: (0, 0)),
        ],
        out_specs=[
            pl.BlockSpec((1, tm, r), lambda bi, i: (bi, i, 0)),
            pl.BlockSpec((1, tm, r), lambda bi, i: (bi, i, 0)),
            pl.BlockSpec((1, r, tm), lambda bi, i: (bi, 0, i)),
            pl.BlockSpec((1, r, tm), lambda bi, i: (bi, 0, i)),
        ],
        out_shape=[nat, nat, tr, tr],
        compiler_params=_cparams(("parallel", "parallel")),
        name="rope_tables",
    )(positions.reshape(b, s, 1), positions.reshape(b, 1, s), inv_freq2.reshape(1, r), inv_freq2.reshape(r, 1))


def _proj_kernel(h_ref, g_ref, wn_ref, wt_ref, qn_ref, kvn_ref, w1_ref, w2_ref, wkn_ref, e_ref, wv_ref,
                 cn_ref, sn_ref, ct_ref, st_ref,
                 qm_ref, km_ref, vm_ref, qd_ref, kd_ref, vd_ref, pin_ref):
    r = MLA_ROPE_DIM
    u = _rms(h_ref[0], g_ref[...]).astype(BF16)
    zn = _dot(u, wn_ref[...])
    zt = _dot_nt(wt_ref[...], u)

    cqn = _rms(zn[:, :MLA_Q_RANK], qn_ref[...]).astype(BF16)
    ckvn = _rms(zn[:, MLA_Q_RANK:MLA_Q_RANK + MLA_KV_RANK], kvn_ref[...]).astype(BF16)
    o = MLA_Q_RANK + MLA_KV_RANK
    kr, krp = zn[:, o:o + r], zn[:, o + r:o + 2 * r]

    qscale = (MLA_NOPE_DIM + MLA_ROPE_DIM) ** -0.5 * LOG2E
    q1 = _dot_nt(w1_ref[...], cqn)
    q2 = _dot_nt(w2_ref[...], cqn)
    ct, st = ct_ref[0], st_ref[0]
    for hd in range(MLA_HEADS):
        base = hd * MLA_HEAD_PAD
        qm_ref[0, 0, base:base + MLA_NOPE_DIM, :] = (q1[base:base + MLA_NOPE_DIM] * qscale).astype(BF16)
        ro = base + MLA_NOPE_DIM
        roped = q1[ro:ro + r] * ct + q2[hd * r:(hd + 1) * r] * st
        qm_ref[0, 0, ro:ro + r, :] = (roped * qscale).astype(BF16)
        qm_ref[0, 0, ro + r:base + MLA_HEAD_PAD, :] = jnp.zeros((MLA_HEAD_PAD - MLA_NOPE_DIM - r, q1.shape[1]), BF16)

    kro = (kr * cn_ref[0] + krp * sn_ref[0]).astype(BF16)
    km_ref[0] = (_dot(ckvn, wkn_ref[...]) + _dot(kro, e_ref[...])).astype(BF16)

    vmt = _dot_nt(wv_ref[...], ckvn).astype(BF16)
    qdt = (zt[:DIFF_HEADS * 2 * DIFF_HEAD_DIM] * (DIFF_HEAD_DIM ** -0.5 * LOG2E)).astype(BF16)
    vdt = zt[DIFF_HEADS * 2 * DIFF_HEAD_DIM:].astype(BF16)
    vm_ref[0, 0] = vmt
    vd_ref[0, 0] = vdt
    qd_ref[0, 0] = qdt
    kd_ref[0] = zn[:, 512:1024].astype(BF16)
    pin_ref[0] = zn[:, 1024:1536]


def _proj(h, g, wts, tabs, tq, tk):
    b, s, d = h.shape
    tm = tk
    assert s % tq == 0 and tq % tm == 0
    qr = tq // tm
    cn, sn, ct, st = tabs
    names = ("wn", "wt", "qn", "kvn", "w1", "w2", "wkn", "e", "wv")
    consts = [wts[n] for n in names]

    def cspec(a):
        return pl.BlockSpec(a.shape, lambda bi, i, _n=a.ndim: (0,) * _n)

    hm, hd_ = MLA_HEADS * MLA_HEAD_PAD, DIFF_HEADS * 2 * DIFF_HEAD_DIM
    vw = MLA_HEADS * MLA_V_DIM
    out_shape = [
        jax.ShapeDtypeStruct((b, s // tq, hm, tq), BF16),
        jax.ShapeDtypeStruct((b, s, hm), BF16),
        jax.ShapeDtypeStruct((b, s // tk, vw, tk), BF16),
        jax.ShapeDtypeStruct((b, s // tq, hd_, tq), BF16),
        jax.ShapeDtypeStruct((b, s, hd_), BF16),
        jax.ShapeDtypeStruct((b, s // tk, hd_, tk), BF16),
        jax.ShapeDtypeStruct((b, s, 512), F32),
    ]
    qmaj = lambda rows: pl.BlockSpec((1, 1, rows, tm), lambda bi, i: (bi, i // qr, 0, i % qr))
    vmaj = lambda rows: pl.BlockSpec((1, 1, rows, tk), lambda bi, i: (bi, i, 0, 0))
    out_specs = [
        qmaj(hm),
        pl.BlockSpec((1, tm, hm), lambda bi, i: (bi, i, 0)),
        vmaj(vw),
        qmaj(hd_),
        pl.BlockSpec((1, tm, hd_), lambda bi, i: (bi, i, 0)),
        vmaj(hd_),
        pl.BlockSpec((1, tm, 512), lambda bi, i: (bi, i, 0)),
    ]
    r = MLA_ROPE_DIM
    in_specs = ([pl.BlockSpec((1, tm, d), lambda bi, i: (bi, i, 0)), pl.BlockSpec((1, d), lambda bi, i: (0, 0))]
                + [cspec(a) for a in consts]
                + [pl.BlockSpec((1, tm, r), lambda bi, i: (bi, i, 0))] * 2
                + [pl.BlockSpec((1, r, tm), lambda bi, i: (bi, 0, i))] * 2)
    return pl.pallas_call(
        _proj_kernel,
        grid=(b, s // tm),
        in_specs=in_specs,
        out_specs=out_specs,
        out_shape=out_shape,
        compiler_params=_cparams(("parallel", "parallel")),
        name="mixer_proj",
    )(h, g.reshape(1, d), *consts, cn, sn, ct, st)


ONES_ROWS = 16


def _tile_rows(i, t):
    return pl.ds(i * t, t) if isinstance(i, int) else pl.ds(pl.multiple_of(i * t, t), t)


def _flash_causal(nq, n_chains, tq, tk, switch_fn, prep_fn, score_fn, value_fn, finish_fn,
                  s_ref, mt_ref, m_ref, l_ref, acc_ref, kstart_fn=None, n_tiles=None):
    dv = acc_ref.shape[1]
    r = tq // tk
    ones = jnp.ones((ONES_ROWS, tk), BF16)
    cw = tq // ATT_COL_SPLIT
    items = [(c, slice(h * cw, (h + 1) * cw)) for c in range(n_chains) for h in range(ATT_COL_SPLIT)]

    def masked_scores(ctx, c, cs, j):
        row = lax.broadcasted_iota(jnp.int32, (tk, cw), 0)
        col = lax.broadcasted_iota(jnp.int32, (tk, cw), 1) + cs.start
        return jnp.where(row + j * tk <= col, score_fn(ctx, c, cs), NEG_INF)

    l_ref[...] = jnp.zeros_like(l_ref)
    acc_ref[...] = jnp.zeros_like(acc_ref)

    def put(c, cs, s):
        s_ref[c, :, cs] = s
        mt_ref[c, :, cs] = jnp.max(s, axis=0, keepdims=True)

    def first_k(qi):
        return 0 if kstart_fn is None else kstart_fn(qi)

    def consume(kj, first, c, cs):
        m_old = jnp.where(first, NEG_INF, m_ref[c, :, cs])
        m_new = jnp.maximum(m_old, mt_ref[c, :, cs])
        alpha = jnp.exp2(m_old - m_new)
        p = jnp.exp2(s_ref[c, :, cs] - m_new).astype(BF16)
        r = _dot(jnp.concatenate([value_fn(kj, c), ones], axis=0), p)
        acc_ref[c, :, cs] = alpha * acc_ref[c, :, cs] + r[:dv]
        l_ref[c, :, cs] = alpha * l_ref[c, :, cs] + r[dv:dv + 1]
        m_ref[c, :, cs] = m_new

    def step(kj, first, qn, kn, diag_next):
        ctx = prep_fn(qn, kn)
        for c, cs in items:
            s_next = score_fn(ctx, c, cs) if diag_next is None else masked_scores(ctx, c, cs, diag_next)
            consume(kj, first, c, cs)
            put(c, cs, s_next)

    if switch_fn is not None:
        switch_fn(0)
    ctx0 = prep_fn(0, 0)
    for c, cs in items:
        put(c, cs, masked_scores(ctx0, c, cs, 0))

    def body(_, carry):
        qi, kj = carry
        first = kj == first_k(qi)
        last = kj == r * qi + (r - 1)
        qn = jnp.where(last, qi + 1, qi)
        kn = jnp.where(last, first_k(jnp.minimum(qi + 1, nq - 1)), kj + 1)

        if switch_fn is not None:
            pl.when(last)(lambda: switch_fn(qn))

        for j in range(r):
            pl.when(kn == r * qn + j)(functools.partial(step, kj, first, qn, kn, j))
        pl.when(kn < r * qn)(functools.partial(step, kj, first, qn, kn, None))

        @pl.when(last)
        def _():
            finish_fn(qi)

        return qn, kn

    if n_tiles is None:
        n_tiles = r * nq * (nq + 1) // 2
    lax.fori_loop(0, n_tiles - 1, body, (jnp.int32(0), jnp.int32(0)))
    k_end = jnp.int32(r * nq - 1)
    for c, cs in items:
        consume(k_end, k_end == first_k(nq - 1), c, cs)
    finish_fn(nq - 1)


def _mla_attn_kernel(q_ref, k_ref, v_ref, o_ref, s_ref, mt_ref, m_ref, l_ref, acc_ref, *, tq, tk, nq):
    hp = MLA_HEAD_PAD

    def prep(qn, kn):
        return qn, pl.multiple_of(kn * tk, tk)

    def scores(ctx, c, cs):
        qn, ks = ctx
        return _dot(k_ref[0, pl.ds(ks, tk), c * hp:(c + 1) * hp], q_ref[0, qn, c * hp:(c + 1) * hp, cs])

    def finish(qi):
        o = jnp.concatenate([acc_ref[c] / l_ref[c] for c in range(2)], axis=0)
        o_ref[0, _tile_rows(qi, tq), :] = o.T.astype(BF16)

    _flash_causal(nq, 2, tq, tk, None, prep, scores,
                  lambda kj, c: v_ref[0, kj, c * MLA_V_DIM:(c + 1) * MLA_V_DIM, :], finish,
                  s_ref, mt_ref, m_ref, l_ref, acc_ref)


def _mla_attention(qm, km, vm):
    b, nq, _, tq = qm.shape
    nk, tk = vm.shape[1], vm.shape[3]
    s = nq * tq
    n_pair = MLA_HEADS // 2
    hp2 = 2 * MLA_HEAD_PAD
    return pl.pallas_call(
        functools.partial(_mla_attn_kernel, tq=tq, tk=tk, nq=nq),
        grid=(b, n_pair),
        in_specs=[
            pl.BlockSpec((1, nq, hp2, tq), lambda bi, p: (bi, 0, p, 0)),
            pl.BlockSpec((1, s, hp2), lambda bi, p: (bi, 0, p)),
            pl.BlockSpec((1, nk, 2 * MLA_V_DIM, tk), lambda bi, p: (bi, 0, p, 0)),
        ],
        out_specs=pl.BlockSpec((1, s, 2 * MLA_V_DIM), lambda bi, p: (bi, 0, p)),
        out_shape=jax.ShapeDtypeStruct((b, s, MLA_HEADS * MLA_V_DIM), BF16),
        scratch_shapes=[pltpu.VMEM((2, tk, tq), F32), pltpu.VMEM((2, 1, tq), F32), pltpu.VMEM((2, 1, tq), F32),
                        pltpu.VMEM((2, 1, tq), F32), pltpu.VMEM((2, MLA_V_DIM, tq), F32)],
        compiler_params=_cparams(("parallel", "parallel")),
        name="mla_attention",
    )(qm, km, vm)


ALIBI_ROWS = 16
ALIBI_SPLIT = 3
POS_DIGIT_BITS = 8


def _bf16_pieces(x):
    pieces = []
    for _ in range(ALIBI_SPLIT):
        p = x.astype(BF16).astype(F32)
        pieces.append(p)
        x = x - p
    return pieces


def _diff_attn_kernel(q_ref, k_ref, v_ref, pq_ref, pk_ref, c_ref, lam_ref, sub_ref, o_ref,
                      w_ref, s_ref, mt_ref, m_ref, l_ref, acc_ref, *, tq, tk, nq, lambda_init, sorted_pos,
                      band=None):
    dh = DIFF_HEAD_DIM
    n_rep = tq // LANES
    lane_tile = lambda v: jnp.concatenate([v] * n_rep, axis=1)

    if sorted_pos:
        w_ref[:, LANES + ALIBI_ROWS:, :] = jnp.zeros((4, LANES - ALIBI_ROWS, tq), BF16)

    def switch(qn):
        qt = q_ref[0, qn]
        row = lax.broadcasted_iota(jnp.int32, qt.shape, 0)
        for c in range(4):
            w_ref[c, 0:LANES, :] = jnp.where((row >= c * dh) & (row < (c + 1) * dh), qt, jnp.zeros_like(qt))
        if sorted_pos:
            relq = pq_ref[0, qn].astype(F32)
            for hh in range(2):
                cp = [lane_tile(c_ref[0, ALIBI_SPLIT * hh + i:ALIBI_SPLIT * hh + i + 1, :])
                      for i in range(ALIBI_SPLIT)]
                tp = _bf16_pieces(-(cp[0] + cp[1] + cp[2]) * relq)
                rows = ([float(1 << POS_DIGIT_BITS) * p for p in cp] + cp + tp
                        + [jnp.zeros((ALIBI_ROWS - 3 * ALIBI_SPLIT, tq), F32)])
                blk = jnp.concatenate(rows, axis=0).astype(BF16)
                for mp in range(2):
                    w_ref[2 * hh + mp, LANES:LANES + ALIBI_ROWS, :] = blk

    def prep(qn, kn):
        ks = pl.multiple_of(kn * tk, tk)
        if sorted_pos:
            return ks, None
        pq = pq_ref[0, qn]
        pk = pk_ref[0, pl.ds(ks, tk), :]
        dist = jnp.concatenate(
            [jnp.abs(pq[:, j * LANES:(j + 1) * LANES] - pk) for j in range(n_rep)], axis=1).astype(F32)
        return ks, [dist * lane_tile(c_ref[0, hh:hh + 1, :]) for hh in range(2)]

    def scores(ctx, c, cs):
        ks, bias = ctx
        if sorted_pos:
            lhs = jnp.concatenate([k_ref[0, pl.ds(ks, tk), :], pk_ref[0, pl.ds(ks, tk), :]], axis=1)
            return _dot(lhs, w_ref[c, :, cs])
        return _dot(k_ref[0, pl.ds(ks, tk), :], w_ref[c, :, cs]) + bias[c // 2][:, cs]

    def finish(qi):
        lam_v = lam_ref[...]
        lam = (jnp.exp(jnp.sum(lam_v[0:1] * lam_v[1:2], axis=1, keepdims=True))
               - jnp.exp(jnp.sum(lam_v[2:3] * lam_v[3:4], axis=1, keepdims=True)) + lambda_init)
        sub = lane_tile(sub_ref[...])
        outs = []
        for hh in range(2):
            c0, c1 = 2 * hh, 2 * hh + 1
            o = acc_ref[c0] / l_ref[c0] - lam * (acc_ref[c1] / l_ref[c1])
            o = o * lax.rsqrt(jnp.mean(o * o, axis=0, keepdims=True) + EPS) * sub
            outs.append(o * (1.0 - lambda_init))
        o_ref[0, _tile_rows(qi, tq), :] = jnp.concatenate(outs, axis=0).T.astype(BF16)

    kstart_fn = n_tiles = None
    if band is not None:
        kstart_ref, ntile_ref = band
        pair = pl.program_id(0) * pl.num_programs(1) + pl.program_id(1)
        kstart_fn = lambda qi: kstart_ref[pair * nq + qi]
        n_tiles = ntile_ref[pair]

    _flash_causal(nq, 4, tq, tk, switch, prep, scores,
                  lambda kj, c: v_ref[0, kj, (c // 2) * DIFF_V_DIM:(c // 2 + 1) * DIFF_V_DIM, :], finish,
                  s_ref, mt_ref, m_ref, l_ref, acc_ref, kstart_fn=kstart_fn, n_tiles=n_tiles)


def _diff_attn_band_kernel(kstart_ref, ntile_ref, *refs, **kw):
    _diff_attn_kernel(*refs, band=(kstart_ref, ntile_ref), **kw)


def _diff_attention_call(qd, kd, vd, pq, pk, cvec, lam_params, sub, lambda_init, sorted_pos, band=None):
    b, nq, _, tq = qd.shape
    nk, tk = vd.shape[1], vd.shape[3]
    s = nq * tq
    n_pair = DIFF_HEADS // 2
    w_rows = 2 * LANES if sorted_pos else LANES

    def spec(shape, imap):
        return pl.BlockSpec(shape, lambda bi, p, *_: imap(bi, p))

    grid_spec = pltpu.PrefetchScalarGridSpec(
        num_scalar_prefetch=0 if band is None else len(band),
        grid=(b, n_pair),
        in_specs=[
            spec((1, nq, LANES, tq), lambda bi, p: (bi, 0, p, 0)),
            spec((1, s, LANES), lambda bi, p: (bi, 0, p)),
            spec((1, nk, 2 * DIFF_V_DIM, tk), lambda bi, p: (bi, 0, p, 0)),
            spec((1, nq, 1, tq), lambda bi, p: (bi, 0, 0, 0)),
            spec((1, s, LANES), lambda bi, p: (bi, 0, 0)),
            spec((1,) + cvec.shape[1:], lambda bi, p: (p, 0, 0)),
            spec((4, DIFF_HEAD_DIM), lambda bi, p: (0, 0)),
            spec((DIFF_V_DIM, LANES), lambda bi, p: (0, 0)),
        ],
        out_specs=spec((1, s, 2 * DIFF_V_DIM), lambda bi, p: (bi, 0, p)),
        scratch_shapes=[pltpu.VMEM((4, w_rows, tq), BF16), pltpu.VMEM((4, tk, tq), F32),
                        pltpu.VMEM((4, 1, tq), F32), pltpu.VMEM((4, 1, tq), F32), pltpu.VMEM((4, 1, tq), F32),
                        pltpu.VMEM((4, DIFF_V_DIM, tq), F32)],
    )
    body = _diff_attn_kernel if band is None else _diff_attn_band_kernel
    return pl.pallas_call(
        functools.partial(body, tq=tq, tk=tk, nq=nq, lambda_init=lambda_init, sorted_pos=sorted_pos),
        grid_spec=grid_spec,
        out_shape=jax.ShapeDtypeStruct((b, s, DIFF_HEADS * DIFF_V_DIM), BF16),
        compiler_params=_cparams(("parallel", "parallel")),
        name="diff_attention_sorted" if sorted_pos else "diff_attention",
    )(*(band or ()), qd, kd, vd, pq, pk, cvec, lam_params, sub)


UNDERFLOW_LOG2 = 160.0


def _diff_band(qd, kd, rel, c, tq, tk):
    b, nq = qd.shape[:2]
    s = kd.shape[1]
    dh = DIFF_HEAD_DIM
    n_pair = DIFF_HEADS // 2
    qsq = jnp.sum(jnp.square(qd.astype(F32)).reshape(b, nq, DIFF_HEADS, 2, dh, tq), axis=4)
    ksq = jnp.sum(jnp.square(kd.astype(F32)).reshape(b, s, DIFF_HEADS, 2, dh), axis=4)
    bound = jnp.max(jnp.sqrt(jnp.max(qsq, axis=(1, 4))) * jnp.sqrt(jnp.max(ksq, axis=1)), axis=-1)
    width = (UNDERFLOW_LOG2 + 2.02 * bound + 1.0) / c.reshape(1, DIFF_HEADS)
    width = jnp.max(width.reshape(b, n_pair, 2), axis=-1)
    first_q = rel[:, ::tq]
    last_k = rel[:, tk - 1::tk]
    nk = last_k.shape[1]
    dmin = (first_q[:, :, None] - last_k[:, None, :]).astype(F32)
    r = tq // tk
    below = jnp.arange(nk)[None, :] < r * jnp.arange(nq)[:, None]
    skip = (dmin[:, None] >= width[:, :, None, None]) & below[None, None]
    kstart = jnp.sum(skip, axis=-1).astype(jnp.int32)
    ntile = jnp.sum(r * jnp.arange(1, nq + 1, dtype=jnp.int32)[None, None, :] - kstart, axis=-1)
    return kstart.reshape(-1), ntile.reshape(-1).astype(jnp.int32)


def _diff_attention(qd, kd, vd, positions, slopes, lam_params, subln, lambda_init):
    b, nq, _, t = qd.shape
    s = nq * t
    n_pair = DIFF_HEADS // 2
    c = (slopes * LOG2E).reshape(n_pair, 2, 1)
    sub = jnp.broadcast_to(subln[:, None], (DIFF_V_DIM, LANES))
    rel = positions - positions[:, :1]
    span = 1 << (2 * POS_DIGIT_BITS)
    sorted_pos = (jnp.all(positions[:, 1:] >= positions[:, :-1]) & jnp.all((rel >= 0) & (rel < span)))

    def sorted_path(_):
        hi = (rel >> POS_DIGIT_BITS).astype(BF16)
        lo = (rel & ((1 << POS_DIGIT_BITS) - 1)).astype(BF16)
        one = jnp.ones_like(hi)
        feats = jnp.stack([hi] * ALIBI_SPLIT + [lo] * ALIBI_SPLIT + [one] * ALIBI_SPLIT, axis=-1)
        feats = jnp.pad(feats, ((0, 0), (0, 0), (0, LANES - feats.shape[-1])))
        pieces = jnp.concatenate(_bf16_pieces(c), axis=-1).reshape(n_pair, 2 * ALIBI_SPLIT, 1)
        cvec = jnp.broadcast_to(jnp.pad(pieces, ((0, 0), (0, 8 - 2 * ALIBI_SPLIT), (0, 0))), (n_pair, 8, LANES))
        band = _diff_band(qd, kd, rel, c, t, vd.shape[3])
        return _diff_attention_call(qd, kd, vd, rel.reshape(b, nq, 1, t), feats, cvec, lam_params, sub,
                                    lambda_init, True, band)

    def general_path(_):
        pk = jnp.broadcast_to(positions[:, :, None], (b, s, LANES))
        cvec = jnp.broadcast_to(-c, (n_pair, 2, LANES))
        return _diff_attention_call(qd, kd, vd, positions.reshape(b, nq, 1, t), pk, cvec, lam_params, sub,
                                    lambda_init, False)

    return lax.cond(sorted_pos, sorted_path, general_path, None)


def _merge_kernel(h_ref, g_ref, ym_ref, yd_ref, pin_ref, halo_ref, wgate_ref, wb_ref, pw_ref, pb_ref, ps_ref,
                  wo_ref, o_ref, ext_ref, *, tm):
    i = pl.program_id(1)
    h = h_ref[0]
    d = h.shape[1]
    u = _rms(h, g_ref[...]).astype(BF16)

    x = pin_ref[0]
    ext_ref[0:POOL_HALO, :] = jnp.where(i == 0, jnp.zeros_like(halo_ref[0]), halo_ref[0])
    ext_ref[POOL_HALO:, :] = x
    tpos = i * tm + lax.broadcasted_iota(jnp.int32, (tm, POOL_GROUP), 0)
    yp = []
    for gi, w in enumerate(POOL_WINDOWS):
        ls = slice(gi * POOL_GROUP, (gi + 1) * POOL_GROUP)
        tot = ext_ref[POOL_HALO:POOL_HALO + tm, ls]
        for j in range(1, w):
            tot = tot + ext_ref[POOL_HALO - j:POOL_HALO - j + tm, ls]
        cnt = jnp.minimum(tpos + 1, w).astype(F32)
        pooled = tot / cnt - x[:, ls]
        yp.append((_dot(pooled.astype(BF16), pw_ref[gi]) + pb_ref[gi:gi + 1, :]) * ps_ref[:, ls])
    y_pool = jnp.concatenate(yp, axis=1).astype(BF16)

    merged = jnp.zeros((tm, d), F32)
    for bi, y in enumerate((ym_ref[0], yd_ref[0], y_pool)):
        gate = jax.nn.sigmoid(_dot(u, wgate_ref[:, bi * d:(bi + 1) * d]))
        merged = merged + gate * _dot(y, wb_ref[bi])
    o_ref[0] = h + _dot(merged.astype(BF16), wo_ref[...])


def _merge(h, g, y_mla, y_diff, pin, wts):
    b, s, d = h.shape
    tm = min(MERGE_TM, s)
    assert s % tm == 0 and tm % POOL_HALO == 0
    hb = tm // POOL_HALO
    consts = [wts[n] for n in ("wgate", "wb", "pw", "pb", "ps", "wo")]

    def cspec(a):
        return pl.BlockSpec(a.shape, lambda bi, i, _n=a.ndim: (0,) * _n)

    tile = lambda w: pl.BlockSpec((1, tm, w), lambda bi, i: (bi, i, 0))
    return pl.pallas_call(
        functools.partial(_merge_kernel, tm=tm),
        grid=(b, s // tm),
        in_specs=[tile(d), pl.BlockSpec((1, d), lambda bi, i: (0, 0)), tile(BRANCH_WIDTH), tile(BRANCH_WIDTH),
                  tile(512),
                  pl.BlockSpec((1, POOL_HALO, 512), lambda bi, i: (bi, jnp.maximum(i * hb - 1, 0), 0))]
                 + [cspec(a) for a in consts],
        out_specs=tile(d),
        out_shape=jax.ShapeDtypeStruct((b, s, d), F32),
        scratch_shapes=[pltpu.VMEM((tm + POOL_HALO, 512), F32)],
        compiler_params=_cparams(("parallel", "parallel")),
        name="gated_merge",
    )(h, g.reshape(1, d), y_mla, y_diff, pin, pin, *consts)


def _rot_half_cols(w):
    half = w.shape[-1] // 2
    return jnp.concatenate([-w[..., half:], w[..., :half]], axis=-1)


def _layer_weights(l, w_in, mla_q_norm, mla_w_uq, mla_kv_norm, mla_w_ukv, pool_w, pool_b, pool_scale,
                   w_branch, w_out):
    d = w_in.shape[1]
    win = w_in[l]
    o_kv = MLA_Q_RANK
    o_kr = o_kv + MLA_KV_RANK
    o_dq = o_kr + MLA_ROPE_DIM
    o_dk = o_dq + 512
    o_dv = o_dk + 512
    o_p = o_dv + 512
    o_g = o_p + 512
    w_kr = win[:, o_kr:o_dq]
    wn = jnp.concatenate([win[:, :o_dq], _rot_half_cols(w_kr), jnp.zeros((d, 64), F32),
                          win[:, o_dk:o_dv], win[:, o_p:o_g]], axis=1)
    wt = jnp.concatenate([win[:, o_dq:o_dk], win[:, o_dv:o_p]], axis=1).T

    wq = mla_w_uq[l].reshape(MLA_Q_RANK, MLA_HEADS, MLA_NOPE_DIM + MLA_ROPE_DIM)
    rope = wq[..., MLA_NOPE_DIM:]
    pad = jnp.zeros((MLA_Q_RANK, MLA_HEADS, MLA_HEAD_PAD - MLA_NOPE_DIM - MLA_ROPE_DIM), F32)
    w1 = jnp.concatenate([wq, pad], axis=-1).reshape(MLA_Q_RANK, MLA_HEADS * MLA_HEAD_PAD).T
    w2 = _rot_half_cols(rope).reshape(MLA_Q_RANK, MLA_HEADS * MLA_ROPE_DIM).T

    wkv = mla_w_ukv[l]
    lane = jnp.arange(wkv.shape[1]) % (MLA_NOPE_DIM + MLA_V_DIM)
    wkn = jnp.where(lane[None, :] < MLA_NOPE_DIM, wkv, 0.0)
    wv = wkv.reshape(MLA_KV_RANK, MLA_HEADS, MLA_NOPE_DIM + MLA_V_DIM)[..., MLA_NOPE_DIM:]
    wv = wv.reshape(MLA_KV_RANK, MLA_HEADS * MLA_V_DIM).T
    col = jnp.arange(MLA_HEADS * MLA_HEAD_PAD)
    e = ((col[None, :] % MLA_HEAD_PAD) == (MLA_NOPE_DIM + jnp.arange(MLA_ROPE_DIM)[:, None])).astype(BF16)

    bf = lambda a: a.astype(BF16)
    return {
        "wn": bf(wn), "wt": bf(wt), "qn": mla_q_norm[l].reshape(1, -1), "kvn": mla_kv_norm[l].reshape(1, -1),
        "w1": bf(w1), "w2": bf(w2), "wkn": bf(wkn), "e": e, "wv": bf(wv),
        "wgate": bf(win[:, o_g:]), "wb": bf(w_branch[l]), "pw": bf(pool_w[l]), "pb": pool_b[l],
        "ps": pool_scale[l].reshape(1, -1), "wo": bf(w_out[l]),
    }


def kernel(x, positions, ffn1_norm, ffn1_w_gate, ffn1_w_up, ffn1_w_down, mix_norm, w_in, mla_q_norm, mla_w_uq, mla_kv_norm, mla_w_ukv, diff_lambda_q1, diff_lambda_k1, diff_lambda_q2, diff_lambda_k2, diff_subln, pool_w, pool_b, pool_scale, w_branch, w_out, ffn2_norm, ffn2_w_gate, ffn2_w_up, ffn2_w_down, final_norm):
    b, s, d = x.shape
    depth = w_in.shape[0]
    tq, tk = min(ATT_TQ, s), min(ATT_TK, s)
    assert s % tq == 0

    half = MLA_ROPE_DIM // 2
    inv_freq = ROPE_BASE ** (-jnp.arange(half, dtype=F32) / half)
    tabs = _rope_tables(positions, jnp.concatenate([inv_freq, inv_freq]))
    slopes = jnp.exp2(-8.0 * jnp.arange(1, DIFF_HEADS + 1, dtype=F32) / DIFF_HEADS)

    bf = lambda a: a.astype(BF16)
    ffn1_w = (bf(ffn1_w_gate), bf(ffn1_w_up), bf(ffn1_w_down))
    ffn2_w = (bf(ffn2_w_gate), bf(ffn2_w_up), bf(ffn2_w_down))
    h = x.reshape(b * s, d)
    for l in range(depth):
        h = _ffn(h, ffn1_norm[l], *ffn1_w, l)
        wts = _layer_weights(l, w_in, mla_q_norm, mla_w_uq, mla_kv_norm, mla_w_ukv, pool_w, pool_b, pool_scale,
                             w_branch, w_out)
        h3 = h.reshape(b, s, d)
        qm, km, vm, qd, kd, vd, pin = _proj(h3, mix_norm[l], wts, tabs, tq, tk)
        y_mla = _mla_attention(qm, km, vm)
        lambda_init = 0.8 - 0.6 * math.exp(-0.3 * l)
        lam_params = jnp.stack([diff_lambda_q1[l], diff_lambda_k1[l], diff_lambda_q2[l], diff_lambda_k2[l]])
        y_diff = _diff_attention(qd, kd, vd, positions, slopes, lam_params, diff_subln[l], lambda_init)
        h = _merge(h3, mix_norm[l], y_mla, y_diff, pin, wts).reshape(b * s, d)
        last = l == depth - 1
        h = _ffn(h, ffn2_norm[l], *ffn2_w, l, final_g=final_norm if last else None)
    return h.reshape(b, s, d)
```

```python
import functools
import math

import jax
import jax.numpy as jnp
from jax import lax
from jax.experimental import pallas as pl
from jax.experimental.pallas import tpu as pltpu

F32 = jnp.float32
BF16 = jnp.bfloat16

EPS = 1e-6
NEG_INF = -1e30
LOG2E = math.log2(math.e)

MLA_HEADS = 8
MLA_NOPE_DIM = 64
MLA_ROPE_DIM = 32
MLA_V_DIM = 64
MLA_Q_RANK = 256
MLA_KV_RANK = 128
ROPE_BASE = 10000.0
MLA_HEAD_PAD = 128
MLA_GROUP = 4

DIFF_HEADS = 8
DIFF_HEAD_DIM = 32
DIFF_V_DIM = 2 * DIFF_HEAD_DIM

POOL_WINDOWS = (2, 4, 8, 16)
POOL_GROUP = 128
POOL_HALO = 16
N_BRANCH = 3
BRANCH_WIDTH = 512

LANES = 128
VMEM_LIMIT_BYTES = 56 * 1024 * 1024

FFN_TM = 1024
FFN_TF = 256
ATT_TQ = 512
ATT_TK = 512
ATT_COL_SPLIT = 1
MERGE_TM = 512
TAB_TM = 1024


def _cparams(sem):
    return pltpu.CompilerParams(dimension_semantics=sem, vmem_limit_bytes=VMEM_LIMIT_BYTES)


def _rms(x, g):
    return x * lax.rsqrt(jnp.mean(x * x, axis=-1, keepdims=True) + EPS) * g


def _dot(a, b):
    return jnp.dot(a, b, preferred_element_type=F32)


def _dot_nt(a, b):
    return lax.dot_general(a, b, (((1,), (1,)), ((), ())), preferred_element_type=F32)


def _ffn_kernel(h_ref, g_ref, wg_ref, wu_ref, wd_ref, *rest, n_f, final):
    if final:
        fg_ref, o_ref, xn_ref, acc_ref = rest
    else:
        o_ref, xn_ref, acc_ref = rest
    f = pl.program_id(1)

    @pl.when(f == 0)
    def _():
        xn_ref[...] = _rms(h_ref[...], g_ref[...]).astype(BF16)
        acc_ref[...] = jnp.zeros_like(acc_ref)

    xn = xn_ref[...]
    a = _dot(xn, wg_ref[...])
    b = _dot(xn, wu_ref[...])
    hid = (a * jax.nn.sigmoid(a)) * b
    acc_ref[...] += _dot(hid.astype(BF16), wd_ref[...])

    @pl.when(f == n_f - 1)
    def _():
        y = h_ref[...] + 0.5 * acc_ref[...]
        if final:
            y = _rms(y, fg_ref[...])
        o_ref[...] = y


def _ffn(h, g, wg, wu, wd, l, final_g=None):
    t, d = h.shape
    ff = wg.shape[2]
    tm, tf = min(FFN_TM, t), FFN_TF
    assert t % tm == 0 and ff % tf == 0
    n_f = ff // tf
    final = final_g is not None
    in_specs = [
        pl.BlockSpec((tm, d), lambda i, f: (i, 0)),
        pl.BlockSpec((1, d), lambda i, f: (0, 0)),
        pl.BlockSpec((None, d, tf), lambda i, f: (l, 0, f)),
        pl.BlockSpec((None, d, tf), lambda i, f: (l, 0, f)),
        pl.BlockSpec((None, tf, d), lambda i, f: (l, f, 0)),
    ]
    args = [h, g.reshape(1, d), wg, wu, wd]
    if final:
        in_specs.append(pl.BlockSpec((1, d), lambda i, f: (0, 0)))
        args.append(final_g.reshape(1, d))
    return pl.pallas_call(
        functools.partial(_ffn_kernel, n_f=n_f, final=final),
        grid=(t // tm, n_f),
        in_specs=in_specs,
        out_specs=pl.BlockSpec((tm, d), lambda i, f: (i, 0)),
        out_shape=jax.ShapeDtypeStruct((t, d), F32),
        scratch_shapes=[pltpu.VMEM((tm, d), BF16), pltpu.VMEM((tm, d), F32)],
        compiler_params=_cparams(("parallel", "arbitrary")),
        name="ffn",
    )(*args)


def _rope_table_kernel(posc_ref, posr_ref, fr_ref, fc_ref, cn_ref, sn_ref, ct_ref, st_ref):
    ang_n = posc_ref[0].astype(F32) * fr_ref[...]
    cn_ref[0] = jnp.cos(ang_n)
    sn_ref[0] = jnp.sin(ang_n)
    ang_t = fc_ref[...] * posr_ref[0].astype(F32)
    ct_ref[0] = jnp.cos(ang_t)
    st_ref[0] = jnp.sin(ang_t)


def _rope_tables(positions, inv_freq2):
    b, s = positions.shape
    r = inv_freq2.shape[0]
    tm = min(TAB_TM, s)
    assert s % tm == 0
    nat = jax.ShapeDtypeStruct((b, s, r), F32)
    tr = jax.ShapeDtypeStruct((b, r, s), F32)
    return pl.pallas_call(
        _rope_table_kernel,
        grid=(b, s // tm),
        in_specs=[
            pl.BlockSpec((1, tm, 1), lambda bi, i: (bi, i, 0)),
            pl.BlockSpec((1, 1, tm), lambda bi, i: (bi, 0, i)),
            pl.BlockSpec((1, r), lambda bi, i: (0, 0)),
            pl.BlockSpec((r, 1), lambda bi, i: (0, 0)),
        ],
        out_specs=[
            pl.BlockSpec((1, tm, r), lambda bi, i: (bi, i, 0)),
            pl.BlockSpec((1, tm, r), lambda bi, i: (bi, i, 0)),
            pl.BlockSpec((1, r, tm), lambda bi, i: (bi, 0, i)),
            pl.BlockSpec((1, r, tm), lambda bi, i: (bi, 0, i)),
        ],
        out_shape=[nat, nat, tr, tr],
        compiler_params=_cparams(("parallel", "parallel")),
        name="rope_tables",
    )(positions.reshape(b, s, 1), positions.reshape(b, 1, s), inv_freq2.reshape(1, r), inv_freq2.reshape(r, 1))


def _proj_kernel(h_ref, g_ref, wn_ref, wt_ref, qn_ref, kvn_ref, w1_ref, w2_ref, wkn_ref, e_ref, wv_ref,
                 cn_ref, sn_ref, ct_ref, st_ref,
                 grp_ref, qm_ref, km_ref, vm_ref, qd_ref, kd_ref, vd_ref, pin_ref, qn2_ref, kn2_ref):
    r = MLA_ROPE_DIM
    u = _rms(h_ref[0], g_ref[...]).astype(BF16)
    zn = _dot(u, wn_ref[...])
    zt = _dot_nt(wt_ref[...], u)

    cqn = _rms(zn[:, :MLA_Q_RANK], qn_ref[...]).astype(BF16)
    ckvn = _rms(zn[:, MLA_Q_RANK:MLA_Q_RANK + MLA_KV_RANK], kvn_ref[...]).astype(BF16)
    o = MLA_Q_RANK + MLA_KV_RANK
    kr, krp = zn[:, o:o + r], zn[:, o + r:o + 2 * r]

    qscale = (MLA_NOPE_DIM + MLA_ROPE_DIM) ** -0.5 * LOG2E
    q1 = _dot_nt(w1_ref[...], cqn)
    q2 = _dot_nt(w2_ref[...], cqn)
    ct, st = ct_ref[0], st_ref[0]
    for hd in range(MLA_HEADS):
        base = hd * MLA_HEAD_PAD
        qm_ref[0, 0, base:base + MLA_NOPE_DIM, :] = (q1[base:base + MLA_NOPE_DIM] * qscale).astype(BF16)
        ro = base + MLA_NOPE_DIM
        roped = q1[ro:ro + r] * ct + q2[hd * r:(hd + 1) * r] * st
        qm_ref[0, 0, ro:ro + r, :] = (roped * qscale).astype(BF16)
        qm_ref[0, 0, ro + r:base + MLA_HEAD_PAD, :] = jnp.zeros((MLA_HEAD_PAD - MLA_NOPE_DIM - r, q1.shape[1]), BF16)

    kro = (kr * cn_ref[0] + krp * sn_ref[0]).astype(BF16)
    km_ref[0] = (_dot(ckvn, wkn_ref[...]) + _dot(kro, e_ref[...])).astype(BF16)

    vmt = _dot_nt(wv_ref[...], ckvn).astype(BF16)
    qdt = (zt[:DIFF_HEADS * 2 * DIFF_HEAD_DIM] * (DIFF_HEAD_DIM ** -0.5 * LOG2E)).astype(BF16)
    vdt = zt[DIFF_HEADS * 2 * DIFF_HEAD_DIM:].astype(BF16)
    vm_ref[0, 0] = vmt
    vd_ref[0, 0] = vdt
    qd_ref[0, 0] = qdt
    kdn = zn[:, 512:1024].astype(BF16)
    kd_ref[0] = kdn
    pin_ref[0] = zn[:, 1024:1536]

    qf, kf = qdt.astype(F32), kdn.astype(F32)
    qsum = _dot(grp_ref[...], (qf * qf).astype(BF16))
    ksum = _dot_nt((kf * kf).astype(BF16), grp_ref[...])
    n_grp = grp_ref.shape[0]
    qn2_ref[0, 0] = jnp.broadcast_to(jnp.max(qsum, axis=1, keepdims=True), (n_grp, LANES))
    kn2_ref[0, 0] = jnp.broadcast_to(jnp.max(ksum, axis=0, keepdims=True), (8, n_grp))


def _proj(h, g, wts, tabs, tq, tk):
    b, s, d = h.shape
    tm = tk
    assert s % tq == 0 and tq % tm == 0
    qr = tq // tm
    cn, sn, ct, st = tabs
    names = ("wn", "wt", "qn", "kvn", "w1", "w2", "wkn", "e", "wv")
    consts = [wts[n] for n in names]

    def cspec(a):
        return pl.BlockSpec(a.shape, lambda bi, i, _n=a.ndim: (0,) * _n)

    hm, hd_ = MLA_HEADS * MLA_HEAD_PAD, DIFF_HEADS * 2 * DIFF_HEAD_DIM
    vw = MLA_HEADS * MLA_V_DIM
    n_grp = DIFF_HEADS * 2
    grp = (jnp.arange(hd_)[None, :] // DIFF_HEAD_DIM == jnp.arange(n_grp)[:, None]).astype(BF16)
    out_shape = [
        jax.ShapeDtypeStruct((b, s // tq, hm, tq), BF16),
        jax.ShapeDtypeStruct((b, s, hm), BF16),
        jax.ShapeDtypeStruct((b, s // tk, vw, tk), BF16),
        jax.ShapeDtypeStruct((b, s // tq, hd_, tq), BF16),
        jax.ShapeDtypeStruct((b, s, hd_), BF16),
        jax.ShapeDtypeStruct((b, s // tk, hd_, tk), BF16),
        jax.ShapeDtypeStruct((b, s, 512), F32),
        jax.ShapeDtypeStruct((b, s // tm, n_grp, LANES), F32),
        jax.ShapeDtypeStruct((b, s // tm, 8, n_grp), F32),
    ]
    qmaj = lambda rows: pl.BlockSpec((1, 1, rows, tm), lambda bi, i: (bi, i // qr, 0, i % qr))
    vmaj = lambda rows: pl.BlockSpec((1, 1, rows, tk), lambda bi, i: (bi, i, 0, 0))
    out_specs = [
        qmaj(hm),
        pl.BlockSpec((1, tm, hm), lambda bi, i: (bi, i, 0)),
        vmaj(vw),
        qmaj(hd_),
        pl.BlockSpec((1, tm, hd_), lambda bi, i: (bi, i, 0)),
        vmaj(hd_),
        pl.BlockSpec((1, tm, 512), lambda bi, i: (bi, i, 0)),
        pl.BlockSpec((1, 1, n_grp, LANES), lambda bi, i: (bi, i, 0, 0)),
        pl.BlockSpec((1, 1, 8, n_grp), lambda bi, i: (bi, i, 0, 0)),
    ]
    r = MLA_ROPE_DIM
    in_specs = ([pl.BlockSpec((1, tm, d), lambda bi, i: (bi, i, 0)), pl.BlockSpec((1, d), lambda bi, i: (0, 0))]
                + [cspec(a) for a in consts]
                + [pl.BlockSpec((1, tm, r), lambda bi, i: (bi, i, 0))] * 2
                + [pl.BlockSpec((1, r, tm), lambda bi, i: (bi, 0, i))] * 2
                + [cspec(grp)])
    return pl.pallas_call(
        _proj_kernel,
        grid=(b, s // tm),
        in_specs=in_specs,
        out_specs=out_specs,
        out_shape=out_shape,
        compiler_params=_cparams(("parallel", "parallel")),
        name="mixer_proj",
    )(h, g.reshape(1, d), *consts, cn, sn, ct, st, grp)


ONES_ROWS = 16


def _tile_rows(i, t):
    return pl.ds(i * t, t) if isinstance(i, int) else pl.ds(pl.multiple_of(i * t, t), t)


def _flash_causal(nq, n_chains, tq, tk, switch_fn, prep_fn, score_fn, value_fn, finish_fn,
                  s_ref, mt_ref, m_ref, l_ref, acc_ref, kstart_fn=None, n_tiles=None):
    dv = acc_ref.shape[1]
    r = tq // tk
    ones = jnp.ones((ONES_ROWS, tk), BF16)
    cw = tq // ATT_COL_SPLIT
    items = [(c, slice(h * cw, (h + 1) * cw)) for c in range(n_chains) for h in range(ATT_COL_SPLIT)]

    def masked_scores(ctx, c, cs, j):
        row = lax.broadcasted_iota(jnp.int32, (tk, cw), 0)
        col = lax.broadcasted_iota(jnp.int32, (tk, cw), 1) + cs.start
        return jnp.where(row + j * tk <= col, score_fn(ctx, c, cs), NEG_INF)

    l_ref[...] = jnp.zeros_like(l_ref)
    acc_ref[...] = jnp.zeros_like(acc_ref)

    def put(c, cs, s):
        s_ref[c, :, cs] = s
        mt_ref[c, :, cs] = jnp.max(s, axis=0, keepdims=True)

    def first_k(qi):
        return 0 if kstart_fn is None else kstart_fn(qi)

    def consume(kj, first, c, cs):
        m_old = jnp.where(first, NEG_INF, m_ref[c, :, cs])
        m_new = jnp.maximum(m_old, mt_ref[c, :, cs])
        alpha = jnp.exp2(m_old - m_new)
        p = jnp.exp2(s_ref[c, :, cs] - m_new).astype(BF16)
        r = _dot(jnp.concatenate([value_fn(kj, c), ones], axis=0), p)
        acc_ref[c, :, cs] = alpha * acc_ref[c, :, cs] + r[:dv]
        l_ref[c, :, cs] = alpha * l_ref[c, :, cs] + r[dv:dv + 1]
        m_ref[c, :, cs] = m_new

    def step(kj, first, qn, kn, diag_next):
        ctx = prep_fn(qn, kn)
        for c, cs in items:
            s_next = score_fn(ctx, c, cs) if diag_next is None else masked_scores(ctx, c, cs, diag_next)
            consume(kj, first, c, cs)
            put(c, cs, s_next)

    if switch_fn is not None:
        switch_fn(0)
    ctx0 = prep_fn(0, 0)
    for c, cs in items:
        put(c, cs, masked_scores(ctx0, c, cs, 0))

    def body(_, carry):
        qi, kj = carry
        first = kj == first_k(qi)
        last = kj == r * qi + (r - 1)
        qn = jnp.where(last, qi + 1, qi)
        kn = jnp.where(last, first_k(jnp.minimum(qi + 1, nq - 1)), kj + 1)

        if switch_fn is not None:
            pl.when(last)(lambda: switch_fn(qn))

        for j in range(r):
            pl.when(kn == r * qn + j)(functools.partial(step, kj, first, qn, kn, j))
        pl.when(kn < r * qn)(functools.partial(step, kj, first, qn, kn, None))

        @pl.when(last)
        def _():
            finish_fn(qi)

        return qn, kn

    if n_tiles is None:
        n_tiles = r * nq * (nq + 1) // 2
    lax.fori_loop(0, n_tiles - 1, body, (jnp.int32(0), jnp.int32(0)))
    k_end = jnp.int32(r * nq - 1)
    for c, cs in items:
        consume(k_end, k_end == first_k(nq - 1), c, cs)
    finish_fn(nq - 1)


def _mla_attn_kernel(q_ref, k_ref, v_ref, o_ref, s_ref, mt_ref, m_ref, l_ref, acc_ref, *, tq, tk, nq):
    hp = MLA_HEAD_PAD

    def prep(qn, kn):
        return qn, pl.multiple_of(kn * tk, tk)

    def scores(ctx, c, cs):
        qn, ks = ctx
        return _dot(k_ref[0, pl.ds(ks, tk), c * hp:(c + 1) * hp], q_ref[0, qn, c * hp:(c + 1) * hp, cs])

    def finish(qi):
        o = jnp.concatenate([acc_ref[c] / l_ref[c] for c in range(MLA_GROUP)], axis=0)
        o_ref[0, _tile_rows(qi, tq), :] = o.T.astype(BF16)

    _flash_causal(nq, MLA_GROUP, tq, tk, None, prep, scores,
                  lambda kj, c: v_ref[0, kj, c * MLA_V_DIM:(c + 1) * MLA_V_DIM, :], finish,
                  s_ref, mt_ref, m_ref, l_ref, acc_ref)


def _mla_attention(qm, km, vm):
    b, nq, _, tq = qm.shape
    nk, tk = vm.shape[1], vm.shape[3]
    s = nq * tq
    g = MLA_GROUP
    hpg = g * MLA_HEAD_PAD
    return pl.pallas_call(
        functools.partial(_mla_attn_kernel, tq=tq, tk=tk, nq=nq),
        grid=(b, MLA_HEADS // g),
        in_specs=[
            pl.BlockSpec((1, nq, hpg, tq), lambda bi, p: (bi, 0, p, 0)),
            pl.BlockSpec((1, s, hpg), lambda bi, p: (bi, 0, p)),
            pl.BlockSpec((1, nk, g * MLA_V_DIM, tk), lambda bi, p: (bi, 0, p, 0)),
        ],
        out_specs=pl.BlockSpec((1, s, g * MLA_V_DIM), lambda bi, p: (bi, 0, p)),
        out_shape=jax.ShapeDtypeStruct((b, s, MLA_HEADS * MLA_V_DIM), BF16),
        scratch_shapes=[pltpu.VMEM((g, tk, tq), F32), pltpu.VMEM((g, 1, tq), F32), pltpu.VMEM((g, 1, tq), F32),
                        pltpu.VMEM((g, 1, tq), F32), pltpu.VMEM((g, MLA_V_DIM, tq), F32)],
        compiler_params=_cparams(("parallel", "parallel")),
        name="mla_attention",
    )(qm, km, vm)


ALIBI_ROWS = 16
ALIBI_SPLIT = 3
POS_DIGIT_BITS = 8


def _bf16_pieces(x):
    pieces = []
    for _ in range(ALIBI_SPLIT):
        p = x.astype(BF16).astype(F32)
        pieces.append(p)
        x = x - p
    return pieces


def _diff_attn_kernel(q_ref, k_ref, v_ref, pq_ref, pk_ref, c_ref, lam_ref, sub_ref, o_ref,
                      w_ref, s_ref, mt_ref, m_ref, l_ref, acc_ref, *, tq, tk, nq, lambda_init, sorted_pos,
                      band=None):
    dh = DIFF_HEAD_DIM
    n_rep = tq // LANES
    lane_tile = lambda v: jnp.concatenate([v] * n_rep, axis=1)

    if sorted_pos:
        w_ref[:, LANES + ALIBI_ROWS:, :] = jnp.zeros((4, LANES - ALIBI_ROWS, tq), BF16)

    def switch(qn):
        qt = q_ref[0, qn]
        row = lax.broadcasted_iota(jnp.int32, qt.shape, 0)
        for c in range(4):
            w_ref[c, 0:LANES, :] = jnp.where((row >= c * dh) & (row < (c + 1) * dh), qt, jnp.zeros_like(qt))
        if sorted_pos:
            relq = pq_ref[0, qn].astype(F32)
            for hh in range(2):
                cp = [lane_tile(c_ref[0, ALIBI_SPLIT * hh + i:ALIBI_SPLIT * hh + i + 1, :])
                      for i in range(ALIBI_SPLIT)]
                tp = _bf16_pieces(-(cp[0] + cp[1] + cp[2]) * relq)
                rows = ([float(1 << POS_DIGIT_BITS) * p for p in cp] + cp + tp
                        + [jnp.zeros((ALIBI_ROWS - 3 * ALIBI_SPLIT, tq), F32)])
                blk = jnp.concatenate(rows, axis=0).astype(BF16)
                for mp in range(2):
                    w_ref[2 * hh + mp, LANES:LANES + ALIBI_ROWS, :] = blk

    def prep(qn, kn):
        ks = pl.multiple_of(kn * tk, tk)
        if sorted_pos:
            return ks, None
        pq = pq_ref[0, qn]
        pk = pk_ref[0, pl.ds(ks, tk), :]
        dist = jnp.concatenate(
            [jnp.abs(pq[:, j * LANES:(j + 1) * LANES] - pk) for j in range(n_rep)], axis=1).astype(F32)
        return ks, [dist * lane_tile(c_ref[0, hh:hh + 1, :]) for hh in range(2)]

    def scores(ctx, c, cs):
        ks, bias = ctx
        if sorted_pos:
            lhs = jnp.concatenate([k_ref[0, pl.ds(ks, tk), :], pk_ref[0, pl.ds(ks, tk), :]], axis=1)
            return _dot(lhs, w_ref[c, :, cs])
        return _dot(k_ref[0, pl.ds(ks, tk), :], w_ref[c, :, cs]) + bias[c // 2][:, cs]

    def finish(qi):
        lam_v = lam_ref[...]
        lam = (jnp.exp(jnp.sum(lam_v[0:1] * lam_v[1:2], axis=1, keepdims=True))
               - jnp.exp(jnp.sum(lam_v[2:3] * lam_v[3:4], axis=1, keepdims=True)) + lambda_init)
        sub = lane_tile(sub_ref[...])
        outs = []
        for hh in range(2):
            c0, c1 = 2 * hh, 2 * hh + 1
            o = acc_ref[c0] / l_ref[c0] - lam * (acc_ref[c1] / l_ref[c1])
            o = o * lax.rsqrt(jnp.mean(o * o, axis=0, keepdims=True) + EPS) * sub
            outs.append(o * (1.0 - lambda_init))
        o_ref[0, _tile_rows(qi, tq), :] = jnp.concatenate(outs, axis=0).T.astype(BF16)

    kstart_fn = n_tiles = None
    if band is not None:
        kstart_ref, ntile_ref = band
        pair = pl.program_id(0) * pl.num_programs(1) + pl.program_id(1)
        kstart_fn = lambda qi: kstart_ref[pair * nq + qi]
        n_tiles = ntile_ref[pair]

    _flash_causal(nq, 4, tq, tk, switch, prep, scores,
                  lambda kj, c: v_ref[0, kj, (c // 2) * DIFF_V_DIM:(c // 2 + 1) * DIFF_V_DIM, :], finish,
                  s_ref, mt_ref, m_ref, l_ref, acc_ref, kstart_fn=kstart_fn, n_tiles=n_tiles)


def _diff_attn_band_kernel(kstart_ref, ntile_ref, *refs, **kw):
    _diff_attn_kernel(*refs, band=(kstart_ref, ntile_ref), **kw)


def _diff_attention_call(qd, kd, vd, pq, pk, cvec, lam_params, sub, lambda_init, sorted_pos, band=None):
    b, nq, _, tq = qd.shape
    nk, tk = vd.shape[1], vd.shape[3]
    s = nq * tq
    n_pair = DIFF_HEADS // 2
    w_rows = 2 * LANES if sorted_pos else LANES

    def spec(shape, imap):
        return pl.BlockSpec(shape, lambda bi, p, *_: imap(bi, p))

    grid_spec = pltpu.PrefetchScalarGridSpec(
        num_scalar_prefetch=0 if band is None else len(band),
        grid=(b, n_pair),
        in_specs=[
            spec((1, nq, LANES, tq), lambda bi, p: (bi, 0, p, 0)),
            spec((1, s, LANES), lambda bi, p: (bi, 0, p)),
            spec((1, nk, 2 * DIFF_V_DIM, tk), lambda bi, p: (bi, 0, p, 0)),
            spec((1, nq, 1, tq), lambda bi, p: (bi, 0, 0, 0)),
            spec((1, s, LANES), lambda bi, p: (bi, 0, 0)),
            spec((1,) + cvec.shape[1:], lambda bi, p: (p, 0, 0)),
            spec((4, DIFF_HEAD_DIM), lambda bi, p: (0, 0)),
            spec((DIFF_V_DIM, LANES), lambda bi, p: (0, 0)),
        ],
        out_specs=spec((1, s, 2 * DIFF_V_DIM), lambda bi, p: (bi, 0, p)),
        scratch_shapes=[pltpu.VMEM((4, w_rows, tq), BF16), pltpu.VMEM((4, tk, tq), F32),
                        pltpu.VMEM((4, 1, tq), F32), pltpu.VMEM((4, 1, tq), F32), pltpu.VMEM((4, 1, tq), F32),
                        pltpu.VMEM((4, DIFF_V_DIM, tq), F32)],
    )
    body = _diff_attn_kernel if band is None else _diff_attn_band_kernel
    return pl.pallas_call(
        functools.partial(body, tq=tq, tk=tk, nq=nq, lambda_init=lambda_init, sorted_pos=sorted_pos),
        grid_spec=grid_spec,
        out_shape=jax.ShapeDtypeStruct((b, s, DIFF_HEADS * DIFF_V_DIM), BF16),
        compiler_params=_cparams(("parallel", "parallel")),
        name="diff_attention_sorted" if sorted_pos else "diff_attention",
    )(*(band or ()), qd, kd, vd, pq, pk, cvec, lam_params, sub)


UNDERFLOW_LOG2 = 160.0


def _diff_band(qn2, kn2, rel, c, tq, tk):
    b = rel.shape[0]
    nq = rel.shape[1] // tq
    n_pair = DIFF_HEADS // 2
    qmax = jnp.max(qn2, axis=(1, 3)).reshape(b, DIFF_HEADS, 2)
    kmax = jnp.max(kn2, axis=(1, 2)).reshape(b, DIFF_HEADS, 2)
    bound = jnp.max(jnp.sqrt(qmax * kmax), axis=-1)
    width = (UNDERFLOW_LOG2 + 2.05 * bound + 1.0) / c.reshape(1, DIFF_HEADS)
    width = jnp.max(width.reshape(b, n_pair, 2), axis=-1)
    first_q = rel[:, ::tq]
    last_k = rel[:, tk - 1::tk]
    nk = last_k.shape[1]
    dmin = (first_q[:, :, None] - last_k[:, None, :]).astype(F32)
    r = tq // tk
    below = jnp.arange(nk)[None, :] < r * jnp.arange(nq)[:, None]
    skip = (dmin[:, None] >= width[:, :, None, None]) & below[None, None]
    kstart = jnp.sum(skip, axis=-1).astype(jnp.int32)
    ntile = jnp.sum(r * jnp.arange(1, nq + 1, dtype=jnp.int32)[None, None, :] - kstart, axis=-1)
    return kstart.reshape(-1), ntile.reshape(-1).astype(jnp.int32)


def _diff_attention(qd, kd, vd, qn2, kn2, positions, slopes, lam_params, subln, lambda_init):
    b, nq, _, t = qd.shape
    s = nq * t
    n_pair = DIFF_HEADS // 2
    c = (slopes * LOG2E).reshape(n_pair, 2, 1)
    sub = jnp.broadcast_to(subln[:, None], (DIFF_V_DIM, LANES))
    rel = positions - positions[:, :1]
    span = 1 << (2 * POS_DIGIT_BITS)
    sorted_pos = (jnp.all(positions[:, 1:] >= positions[:, :-1]) & jnp.all((rel >= 0) & (rel < span)))

    def sorted_path(_):
        hi = (rel >> POS_DIGIT_BITS).astype(BF16)
        lo = (rel & ((1 << POS_DIGIT_BITS) - 1)).astype(BF16)
        one = jnp.ones_like(hi)
        feats = jnp.stack([hi] * ALIBI_SPLIT + [lo] * ALIBI_SPLIT + [one] * ALIBI_SPLIT, axis=-1)
        feats = jnp.pad(feats, ((0, 0), (0, 0), (0, LANES - feats.shape[-1])))
        pieces = jnp.concatenate(_bf16_pieces(c), axis=-1).reshape(n_pair, 2 * ALIBI_SPLIT, 1)
        cvec = jnp.broadcast_to(jnp.pad(pieces, ((0, 0), (0, 8 - 2 * ALIBI_SPLIT), (0, 0))), (n_pair, 8, LANES))
        band = _diff_band(qn2, kn2, rel, c, t, vd.shape[3])
        return _diff_attention_call(qd, kd, vd, rel.reshape(b, nq, 1, t), feats, cvec, lam_params, sub,
                                    lambda_init, True, band)

    def general_path(_):
        pk = jnp.broadcast_to(positions[:, :, None], (b, s, LANES))
        cvec = jnp.broadcast_to(-c, (n_pair, 2, LANES))
        return _diff_attention_call(qd, kd, vd, positions.reshape(b, nq, 1, t), pk, cvec, lam_params, sub,
                                    lambda_init, False)

    return lax.cond(sorted_pos, sorted_path, general_path, None)


def _merge_kernel(h_ref, g_ref, ym_ref, yd_ref, pin_ref, halo_ref, wgate_ref, wb_ref, pw_ref, pb_ref, ps_ref,
                  wo_ref, o_ref, ext_ref, *, tm):
    i = pl.program_id(1)
    h = h_ref[0]
    d = h.shape[1]
    u = _rms(h, g_ref[...]).astype(BF16)

    x = pin_ref[0]
    ext_ref[0:POOL_HALO, :] = jnp.where(i == 0, jnp.zeros_like(halo_ref[0]), halo_ref[0])
    ext_ref[POOL_HALO:, :] = x
    tpos = i * tm + lax.broadcasted_iota(jnp.int32, (tm, POOL_GROUP), 0)
    yp = []
    for gi, w in enumerate(POOL_WINDOWS):
        ls = slice(gi * POOL_GROUP, (gi + 1) * POOL_GROUP)
        tot = ext_ref[POOL_HALO:POOL_HALO + tm, ls]
        for j in range(1, w):
            tot = tot + ext_ref[POOL_HALO - j:POOL_HALO - j + tm, ls]
        cnt = jnp.minimum(tpos + 1, w).astype(F32)
        pooled = tot / cnt - x[:, ls]
        yp.append((_dot(pooled.astype(BF16), pw_ref[gi]) + pb_ref[gi:gi + 1, :]) * ps_ref[:, ls])
    y_pool = jnp.concatenate(yp, axis=1).astype(BF16)

    merged = jnp.zeros((tm, d), F32)
    for bi, y in enumerate((ym_ref[0], yd_ref[0], y_pool)):
        gate = jax.nn.sigmoid(_dot(u, wgate_ref[:, bi * d:(bi + 1) * d]))
        merged = merged + gate * _dot(y, wb_ref[bi])
    o_ref[0] = h + _dot(merged.astype(BF16), wo_ref[...])


def _merge(h, g, y_mla, y_diff, pin, wts):
    b, s, d = h.shape
    tm = min(MERGE_TM, s)
    assert s % tm == 0 and tm % POOL_HALO == 0
    hb = tm // POOL_HALO
    consts = [wts[n] for n in ("wgate", "wb", "pw", "pb", "ps", "wo")]

    def cspec(a):
        return pl.BlockSpec(a.shape, lambda bi, i, _n=a.ndim: (0,) * _n)

    tile = lambda w: pl.BlockSpec((1, tm, w), lambda bi, i: (bi, i, 0))
    return pl.pallas_call(
        functools.partial(_merge_kernel, tm=tm),
        grid=(b, s // tm),
        in_specs=[tile(d), pl.BlockSpec((1, d), lambda bi, i: (0, 0)), tile(BRANCH_WIDTH), tile(BRANCH_WIDTH),
                  tile(512),
                  pl.BlockSpec((1, POOL_HALO, 512), lambda bi, i: (bi, jnp.maximum(i * hb - 1, 0), 0))]
                 + [cspec(a) for a in consts],
        out_specs=tile(d),
        out_shape=jax.ShapeDtypeStruct((b, s, d), F32),
        scratch_shapes=[pltpu.VMEM((tm + POOL_HALO, 512), F32)],
        compiler_params=_cparams(("parallel", "parallel")),
        name="gated_merge",
    )(h, g.reshape(1, d), y_mla, y_diff, pin, pin, *consts)


def _rot_half_cols(w):
    half = w.shape[-1] // 2
    return jnp.concatenate([-w[..., half:], w[..., :half]], axis=-1)


def _layer_weights(l, w_in, mla_q_norm, mla_w_uq, mla_kv_norm, mla_w_ukv, pool_w, pool_b, pool_scale,
                   w_branch, w_out):
    d = w_in.shape[1]
    win = w_in[l]
    o_kv = MLA_Q_RANK
    o_kr = o_kv + MLA_KV_RANK
    o_dq = o_kr + MLA_ROPE_DIM
    o_dk = o_dq + 512
    o_dv = o_dk + 512
    o_p = o_dv + 512
    o_g = o_p + 512
    w_kr = win[:, o_kr:o_dq]
    wn = jnp.concatenate([win[:, :o_dq], _rot_half_cols(w_kr), jnp.zeros((d, 64), F32),
                          win[:, o_dk:o_dv], win[:, o_p:o_g]], axis=1)
    wt = jnp.concatenate([win[:, o_dq:o_dk], win[:, o_dv:o_p]], axis=1).T

    wq = mla_w_uq[l].reshape(MLA_Q_RANK, MLA_HEADS, MLA_NOPE_DIM + MLA_ROPE_DIM)
    rope = wq[..., MLA_NOPE_DIM:]
    pad = jnp.zeros((MLA_Q_RANK, MLA_HEADS, MLA_HEAD_PAD - MLA_NOPE_DIM - MLA_ROPE_DIM), F32)
    w1 = jnp.concatenate([wq, pad], axis=-1).reshape(MLA_Q_RANK, MLA_HEADS * MLA_HEAD_PAD).T
    w2 = _rot_half_cols(rope).reshape(MLA_Q_RANK, MLA_HEADS * MLA_ROPE_DIM).T

    wkv = mla_w_ukv[l]
    lane = jnp.arange(wkv.shape[1]) % (MLA_NOPE_DIM + MLA_V_DIM)
    wkn = jnp.where(lane[None, :] < MLA_NOPE_DIM, wkv, 0.0)
    wv = wkv.reshape(MLA_KV_RANK, MLA_HEADS, MLA_NOPE_DIM + MLA_V_DIM)[..., MLA_NOPE_DIM:]
    wv = wv.reshape(MLA_KV_RANK, MLA_HEADS * MLA_V_DIM).T
    col = jnp.arange(MLA_HEADS * MLA_HEAD_PAD)
    e = ((col[None, :] % MLA_HEAD_PAD) == (MLA_NOPE_DIM + jnp.arange(MLA_ROPE_DIM)[:, None])).astype(BF16)

    bf = lambda a: a.astype(BF16)
    return {
        "wn": bf(wn), "wt": bf(wt), "qn": mla_q_norm[l].reshape(1, -1), "kvn": mla_kv_norm[l].reshape(1, -1),
        "w1": bf(w1), "w2": bf(w2), "wkn": bf(wkn), "e": e, "wv": bf(wv),
        "wgate": bf(win[:, o_g:]), "wb": bf(w_branch[l]), "pw": bf(pool_w[l]), "pb": pool_b[l],
        "ps": pool_scale[l].reshape(1, -1), "wo": bf(w_out[l]),
    }


def kernel(x, positions, ffn1_norm, ffn1_w_gate, ffn1_w_up, ffn1_w_down, mix_norm, w_in, mla_q_norm, mla_w_uq, mla_kv_norm, mla_w_ukv, diff_lambda_q1, diff_lambda_k1, diff_lambda_q2, diff_lambda_k2, diff_subln, pool_w, pool_b, pool_scale, w_branch, w_out, ffn2_norm, ffn2_w_gate, ffn2_w_up, ffn2_w_down, final_norm):
    b, s, d = x.shape
    depth = w_in.shape[0]
    tq, tk = min(ATT_TQ, s), min(ATT_TK, s)
    assert s % tq == 0

    half = MLA_ROPE_DIM // 2
    inv_freq = ROPE_BASE ** (-jnp.arange(half, dtype=F32) / half)
    tabs = _rope_tables(positions, jnp.concatenate([inv_freq, inv_freq]))
    slopes = jnp.exp2(-8.0 * jnp.arange(1, DIFF_HEADS + 1, dtype=F32) / DIFF_HEADS)

    bf = lambda a: a.astype(BF16)
    ffn1_w = (bf(ffn1_w_gate), bf(ffn1_w_up), bf(ffn1_w_down))
    ffn2_w = (bf(ffn2_w_gate), bf(ffn2_w_up), bf(ffn2_w_down))
    h = x.reshape(b * s, d)
    for l in range(depth):
        h = _ffn(h, ffn1_norm[l], *ffn1_w, l)
        wts = _layer_weights(l, w_in, mla_q_norm, mla_w_uq, mla_kv_norm, mla_w_ukv, pool_w, pool_b, pool_scale,
                             w_branch, w_out)
        h3 = h.reshape(b, s, d)
        qm, km, vm, qd, kd, vd, pin, qn2, kn2 = _proj(h3, mix_norm[l], wts, tabs, tq, tk)
        y_mla = _mla_attention(qm, km, vm)
        lambda_init = 0.8 - 0.6 * math.exp(-0.3 * l)
        lam_params = jnp.stack([diff_lambda_q1[l], diff_lambda_k1[l], diff_lambda_q2[l], diff_lambda_k2[l]])
        y_diff = _diff_attention(qd, kd, vd, qn2, kn2, positions, slopes, lam_params, diff_subln[l], lambda_init)
        h = _merge(h3, mix_norm[l], y_mla, y_diff, pin, wts).reshape(b * s, d)
        last = l == depth - 1
        h = _ffn(h, ffn2_norm[l], *ffn2_w, l, final_g=final_norm if last else None)
    return h.reshape(b, s, d)
```

```python
import functools
import math

import jax
import jax.numpy as jnp
from jax import lax
from jax.experimental import pallas as pl
from jax.experimental.pallas import tpu as pltpu

F32 = jnp.float32
BF16 = jnp.bfloat16

EPS = 1e-6
NEG_INF = -1e30
LOG2E = math.log2(math.e)

MLA_HEADS = 8
MLA_NOPE_DIM = 64
MLA_ROPE_DIM = 32
MLA_V_DIM = 64
MLA_Q_RANK = 256
MLA_KV_RANK = 128
ROPE_BASE = 10000.0
MLA_HEAD_PAD = 128
MLA_GROUP = 4

DIFF_HEADS = 8
DIFF_HEAD_DIM = 32
DIFF_V_DIM = 2 * DIFF_HEAD_DIM

POOL_WINDOWS = (2, 4, 8, 16)
POOL_GROUP = 128
POOL_HALO = 16
N_BRANCH = 3
BRANCH_WIDTH = 512

LANES = 128
VMEM_LIMIT_BYTES = 56 * 1024 * 1024

FFN_TM = 2048
FFN_TF = 256
ATT_TQ = 512
ATT_TK = 512
ATT_COL_SPLIT = 1
MERGE_TM = 512
TAB_TM = 1024


def _cparams(sem):
    return pltpu.CompilerParams(dimension_semantics=sem, vmem_limit_bytes=VMEM_LIMIT_BYTES)


def _rms(x, g):
    return x * lax.rsqrt(jnp.mean(x * x, axis=-1, keepdims=True) + EPS) * g


def _dot(a, b):
    return jnp.dot(a, b, preferred_element_type=F32)


def _dot_nt(a, b):
    return lax.dot_general(a, b, (((1,), (1,)), ((), ())), preferred_element_type=F32)


def _ffn_kernel(h_ref, g_ref, wg_ref, wu_ref, wd_ref, *rest, n_f, final):
    if final:
        fg_ref, o_ref, xn_ref, acc_ref = rest
    else:
        o_ref, xn_ref, acc_ref = rest
    f = pl.program_id(1)

    @pl.when(f == 0)
    def _():
        xn_ref[...] = _rms(h_ref[...], g_ref[...]).astype(BF16)
        acc_ref[...] = jnp.zeros_like(acc_ref)

    xn = xn_ref[...]
    a = _dot(xn, wg_ref[...])
    b = _dot(xn, wu_ref[...])
    hid = (a * jax.nn.sigmoid(a)) * b
    acc_ref[...] += _dot(hid.astype(BF16), wd_ref[...])

    @pl.when(f == n_f - 1)
    def _():
        y = h_ref[...] + 0.5 * acc_ref[...]
        if final:
            y = _rms(y, fg_ref[...])
        o_ref[...] = y


def _ffn(h, g, wg, wu, wd, l, final_g=None):
    t, d = h.shape
    ff = wg.shape[2]
    tm, tf = min(FFN_TM, t), FFN_TF
    assert t % tm == 0 and ff % tf == 0
    n_f = ff // tf
    final = final_g is not None
    in_specs = [
        pl.BlockSpec((tm, d), lambda i, f: (i, 0)),
        pl.BlockSpec((1, d), lambda i, f: (0, 0)),
        pl.BlockSpec((None, d, tf), lambda i, f: (l, 0, f)),
        pl.BlockSpec((None, d, tf), lambda i, f: (l, 0, f)),
        pl.BlockSpec((None, tf, d), lambda i, f: (l, f, 0)),
    ]
    args = [h, g.reshape(1, d), wg, wu, wd]
    if final:
        in_specs.append(pl.BlockSpec((1, d), lambda i, f: (0, 0)))
        args.append(final_g.reshape(1, d))
    return pl.pallas_call(
        functools.partial(_ffn_kernel, n_f=n_f, final=final),
        grid=(t // tm, n_f),
        in_specs=in_specs,
        out_specs=pl.BlockSpec((tm, d), lambda i, f: (i, 0)),
        out_shape=jax.ShapeDtypeStruct((t, d), F32),
        scratch_shapes=[pltpu.VMEM((tm, d), BF16), pltpu.VMEM((tm, d), F32)],
        compiler_params=_cparams(("parallel", "arbitrary")),
        name="ffn",
    )(*args)


def _rope_table_kernel(posc_ref, posr_ref, fr_ref, fc_ref, cn_ref, sn_ref, ct_ref, st_ref):
    ang_n = posc_ref[0].astype(F32) * fr_ref[...]
    cn_ref[0] = jnp.cos(ang_n)
    sn_ref[0] = jnp.sin(ang_n)
    ang_t = fc_ref[...] * posr_ref[0].astype(F32)
    ct_ref[0] = jnp.cos(ang_t)
    st_ref[0] = jnp.sin(ang_t)


def _rope_tables(positions, inv_freq2):
    b, s = positions.shape
    r = inv_freq2.shape[0]
    tm = min(TAB_TM, s)
    assert s % tm == 0
    nat = jax.ShapeDtypeStruct((b, s, r), F32)
    tr = jax.ShapeDtypeStruct((b, r, s), F32)
    return pl.pallas_call(
        _rope_table_kernel,
        grid=(b, s // tm),
        in_specs=[
            pl.BlockSpec((1, tm, 1), lambda bi, i: (bi, i, 0)),
            pl.BlockSpec((1, 1, tm), lambda bi, i: (bi, 0, i)),
            pl.BlockSpec((1, r), lambda bi, i: (0, 0)),
            pl.BlockSpec((r, 1), lambda bi, i: (0, 0)),
        ],
        out_specs=[
            pl.BlockSpec((1, tm, r), lambda bi, i: (bi, i, 0)),
            pl.BlockSpec((1, tm, r), lambda bi, i: (bi, i, 0)),
            pl.BlockSpec((1, r, tm), lambda bi, i: (bi, 0, i)),
            pl.BlockSpec((1, r, tm), lambda bi, i: (bi, 0, i)),
        ],
        out_shape=[nat, nat, tr, tr],
        compiler_params=_cparams(("parallel", "parallel")),
        name="rope_tables",
    )(positions.reshape(b, s, 1), positions.reshape(b, 1, s), inv_freq2.reshape(1, r), inv_freq2.reshape(r, 1))


def _proj_kernel(h_ref, g_ref, wn_ref, wt_ref, qn_ref, kvn_ref, w1_ref, w2_ref, wkn_ref, e_ref, wv_ref,
                 cn_ref, sn_ref, ct_ref, st_ref,
                 grp_ref, qm_ref, km_ref, vm_ref, qd_ref, kd_ref, vd_ref, pin_ref, qn2_ref, kn2_ref):
    r = MLA_ROPE_DIM
    u = _rms(h_ref[0], g_ref[...]).astype(BF16)
    zn = _dot(u, wn_ref[...])
    zt = _dot_nt(wt_ref[...], u)

    cqn = _rms(zn[:, :MLA_Q_RANK], qn_ref[...]).astype(BF16)
    ckvn = _rms(zn[:, MLA_Q_RANK:MLA_Q_RANK + MLA_KV_RANK], kvn_ref[...]).astype(BF16)
    o = MLA_Q_RANK + MLA_KV_RANK
    kr, krp = zn[:, o:o + r], zn[:, o + r:o + 2 * r]

    qscale = (MLA_NOPE_DIM + MLA_ROPE_DIM) ** -0.5 * LOG2E
    q1 = _dot_nt(w1_ref[...], cqn)
    q2 = _dot_nt(w2_ref[...], cqn)
    ct, st = ct_ref[0], st_ref[0]
    for hd in range(MLA_HEADS):
        base = hd * MLA_HEAD_PAD
        qm_ref[0, 0, base:base + MLA_NOPE_DIM, :] = (q1[base:base + MLA_NOPE_DIM] * qscale).astype(BF16)
        ro = base + MLA_NOPE_DIM
        roped = q1[ro:ro + r] * ct + q2[hd * r:(hd + 1) * r] * st
        qm_ref[0, 0, ro:ro + r, :] = (roped * qscale).astype(BF16)
        qm_ref[0, 0, ro + r:base + MLA_HEAD_PAD, :] = jnp.zeros((MLA_HEAD_PAD - MLA_NOPE_DIM - r, q1.shape[1]), BF16)

    kro = (kr * cn_ref[0] + krp * sn_ref[0]).astype(BF16)
    km_ref[0] = (_dot(ckvn, wkn_ref[...]) + _dot(kro, e_ref[...])).astype(BF16)

    vmt = _dot_nt(wv_ref[...], ckvn).astype(BF16)
    qdt = (zt[:DIFF_HEADS * 2 * DIFF_HEAD_DIM] * (DIFF_HEAD_DIM ** -0.5 * LOG2E)).astype(BF16)
    vdt = zt[DIFF_HEADS * 2 * DIFF_HEAD_DIM:].astype(BF16)
    vm_ref[0, 0] = vmt
    vd_ref[0, 0] = vdt
    qd_ref[0, 0] = qdt
    kdn = zn[:, 512:1024].astype(BF16)
    kd_ref[0] = kdn
    pin_ref[0] = zn[:, 1024:1536]

    qf, kf = qdt.astype(F32), kdn.astype(F32)
    qsum = _dot(grp_ref[...], (qf * qf).astype(BF16))
    ksum = _dot_nt((kf * kf).astype(BF16), grp_ref[...])
    n_grp = grp_ref.shape[0]
    qn2_ref[0, 0] = jnp.broadcast_to(jnp.max(qsum, axis=1, keepdims=True), (n_grp, LANES))
    kn2_ref[0, 0] = jnp.broadcast_to(jnp.max(ksum, axis=0, keepdims=True), (8, n_grp))


def _proj(h, g, wts, tabs, tq, tk):
    b, s, d = h.shape
    tm = tk
    assert s % tq == 0 and tq % tm == 0
    qr = tq // tm
    cn, sn, ct, st = tabs
    names = ("wn", "wt", "qn", "kvn", "w1", "w2", "wkn", "e", "wv")
    consts = [wts[n] for n in names]

    def cspec(a):
        return pl.BlockSpec(a.shape, lambda bi, i, _n=a.ndim: (0,) * _n)

    hm, hd_ = MLA_HEADS * MLA_HEAD_PAD, DIFF_HEADS * 2 * DIFF_HEAD_DIM
    vw = MLA_HEADS * MLA_V_DIM
    n_grp = DIFF_HEADS * 2
    grp = (jnp.arange(hd_)[None, :] // DIFF_HEAD_DIM == jnp.arange(n_grp)[:, None]).astype(BF16)
    out_shape = [
        jax.ShapeDtypeStruct((b, s // tq, hm, tq), BF16),
        jax.ShapeDtypeStruct((b, s, hm), BF16),
        jax.ShapeDtypeStruct((b, s // tk, vw, tk), BF16),
        jax.ShapeDtypeStruct((b, s // tq, hd_, tq), BF16),
        jax.ShapeDtypeStruct((b, s, hd_), BF16),
        jax.ShapeDtypeStruct((b, s // tk, hd_, tk), BF16),
        jax.ShapeDtypeStruct((b, s, 512), F32),
        jax.ShapeDtypeStruct((b, s // tm, n_grp, LANES), F32),
        jax.ShapeDtypeStruct((b, s // tm, 8, n_grp), F32),
    ]
    qmaj = lambda rows: pl.BlockSpec((1, 1, rows, tm), lambda bi, i: (bi, i // qr, 0, i % qr))
    vmaj = lambda rows: pl.BlockSpec((1, 1, rows, tk), lambda bi, i: (bi, i, 0, 0))
    out_specs = [
        qmaj(hm),
        pl.BlockSpec((1, tm, hm), lambda bi, i: (bi, i, 0)),
        vmaj(vw),
        qmaj(hd_),
        pl.BlockSpec((1, tm, hd_), lambda bi, i: (bi, i, 0)),
        vmaj(hd_),
        pl.BlockSpec((1, tm, 512), lambda bi, i: (bi, i, 0)),
        pl.BlockSpec((1, 1, n_grp, LANES), lambda bi, i: (bi, i, 0, 0)),
        pl.BlockSpec((1, 1, 8, n_grp), lambda bi, i: (bi, i, 0, 0)),
    ]
    r = MLA_ROPE_DIM
    in_specs = ([pl.BlockSpec((1, tm, d), lambda bi, i: (bi, i, 0)), pl.BlockSpec((1, d), lambda bi, i: (0, 0))]
                + [cspec(a) for a in consts]
                + [pl.BlockSpec((1, tm, r), lambda bi, i: (bi, i, 0))] * 2
                + [pl.BlockSpec((1, r, tm), lambda bi, i: (bi, 0, i))] * 2
                + [cspec(grp)])
    return pl.pallas_call(
        _proj_kernel,
        grid=(b, s // tm),
        in_specs=in_specs,
        out_specs=out_specs,
        out_shape=out_shape,
        compiler_params=_cparams(("parallel", "parallel")),
        name="mixer_proj",
    )(h, g.reshape(1, d), *consts, cn, sn, ct, st, grp)


ONES_ROWS = 16


def _tile_rows(i, t):
    return pl.ds(i * t, t) if isinstance(i, int) else pl.ds(pl.multiple_of(i * t, t), t)


def _flash_causal(nq, n_chains, tq, tk, switch_fn, prep_fn, score_fn, value_fn, finish_fn,
                  s_ref, mt_ref, m_ref, l_ref, acc_ref, kstart_fn=None, n_tiles=None):
    dv = acc_ref.shape[1]
    r = tq // tk
    ones = jnp.ones((ONES_ROWS, tk), BF16)
    cw = tq // ATT_COL_SPLIT
    items = [(c, slice(h * cw, (h + 1) * cw)) for c in range(n_chains) for h in range(ATT_COL_SPLIT)]

    def masked_scores(ctx, c, cs, j):
        row = lax.broadcasted_iota(jnp.int32, (tk, cw), 0)
        col = lax.broadcasted_iota(jnp.int32, (tk, cw), 1) + cs.start
        return jnp.where(row + j * tk <= col, score_fn(ctx, c, cs), NEG_INF)

    m_ref[...] = jnp.full(m_ref.shape, NEG_INF, F32)
    l_ref[...] = jnp.zeros_like(l_ref)
    acc_ref[...] = jnp.zeros_like(acc_ref)

    def put(c, cs, s):
        s_ref[c, :, cs] = s
        mt_ref[c, :, cs] = jnp.max(s, axis=0, keepdims=True)

    def first_k(qi):
        return 0 if kstart_fn is None else kstart_fn(qi)

    def consume(kj, first, c, cs):
        m_old = jnp.where(first, NEG_INF, m_ref[c, :, cs])
        m_new = jnp.maximum(m_old, mt_ref[c, :, cs])
        alpha = jnp.exp2(m_old - m_new)
        p = jnp.exp2((s_ref[c, :, cs] - m_new).astype(BF16))
        r = _dot(jnp.concatenate([value_fn(kj, c), ones], axis=0), p)
        acc_ref[c, :, cs] = alpha * acc_ref[c, :, cs] + r[:dv]
        l_ref[c, :, cs] = alpha * l_ref[c, :, cs] + r[dv:dv + 1]
        m_ref[c, :, cs] = m_new

    def step(kj, first, qn, kn, diag_next):
        ctx = prep_fn(qn, kn)
        for c, cs in items:
            s_next = score_fn(ctx, c, cs) if diag_next is None else masked_scores(ctx, c, cs, diag_next)
            consume(kj, first, c, cs)
            put(c, cs, s_next)

    if switch_fn is not None:
        switch_fn(0)
    ctx0 = prep_fn(0, 0)
    for c, cs in items:
        put(c, cs, masked_scores(ctx0, c, cs, 0))

    def body(_, carry):
        qi, kj = carry
        first = kj == first_k(qi)
        last = kj == r * qi + (r - 1)
        qn = jnp.where(last, qi + 1, qi)
        kn = jnp.where(last, first_k(jnp.minimum(qi + 1, nq - 1)), kj + 1)

        if switch_fn is not None:
            pl.when(last)(lambda: switch_fn(qn))

        for j in range(r):
            pl.when(kn == r * qn + j)(functools.partial(step, kj, first, qn, kn, j))
        pl.when(kn < r * qn)(functools.partial(step, kj, first, qn, kn, None))

        @pl.when(last)
        def _():
            finish_fn(qi)

        return qn, kn

    if n_tiles is None:
        n_tiles = r * nq * (nq + 1) // 2
    lax.fori_loop(0, n_tiles - 1, body, (jnp.int32(0), jnp.int32(0)))
    k_end = jnp.int32(r * nq - 1)
    for c, cs in items:
        consume(k_end, k_end == first_k(nq - 1), c, cs)
    finish_fn(nq - 1)


def _mla_attn_kernel(q_ref, k_ref, v_ref, o_ref, s_ref, mt_ref, m_ref, l_ref, acc_ref, *, tq, tk, nq):
    hp = MLA_HEAD_PAD

    def prep(qn, kn):
        return qn, pl.multiple_of(kn * tk, tk)

    def scores(ctx, c, cs):
        qn, ks = ctx
        return _dot(k_ref[0, pl.ds(ks, tk), c * hp:(c + 1) * hp], q_ref[0, qn, c * hp:(c + 1) * hp, cs])

    def finish(qi):
        o = jnp.concatenate([acc_ref[c] / l_ref[c] for c in range(MLA_GROUP)], axis=0)
        o_ref[0, _tile_rows(qi, tq), :] = o.T.astype(BF16)

    _flash_causal(nq, MLA_GROUP, tq, tk, None, prep, scores,
                  lambda kj, c: v_ref[0, kj, c * MLA_V_DIM:(c + 1) * MLA_V_DIM, :], finish,
                  s_ref, mt_ref, m_ref, l_ref, acc_ref)


def _mla_attention(qm, km, vm):
    b, nq, _, tq = qm.shape
    nk, tk = vm.shape[1], vm.shape[3]
    s = nq * tq
    g = MLA_GROUP
    hpg = g * MLA_HEAD_PAD
    return pl.pallas_call(
        functools.partial(_mla_attn_kernel, tq=tq, tk=tk, nq=nq),
        grid=(b, MLA_HEADS // g),
        in_specs=[
            pl.BlockSpec((1, nq, hpg, tq), lambda bi, p: (bi, 0, p, 0)),
            pl.BlockSpec((1, s, hpg), lambda bi, p: (bi, 0, p)),
            pl.BlockSpec((1, nk, g * MLA_V_DIM, tk), lambda bi, p: (bi, 0, p, 0)),
        ],
        out_specs=pl.BlockSpec((1, s, g * MLA_V_DIM), lambda bi, p: (bi, 0, p)),
        out_shape=jax.ShapeDtypeStruct((b, s, MLA_HEADS * MLA_V_DIM), BF16),
        scratch_shapes=[pltpu.VMEM((g, tk, tq), F32), pltpu.VMEM((g, 1, tq), F32), pltpu.VMEM((g, 1, tq), F32),
                        pltpu.VMEM((g, 1, tq), F32), pltpu.VMEM((g, MLA_V_DIM, tq), F32)],
        compiler_params=_cparams(("parallel", "parallel")),
        name="mla_attention",
    )(qm, km, vm)


ALIBI_ROWS = 16
ALIBI_SPLIT = 3
POS_DIGIT_BITS = 8


def _bf16_pieces(x):
    pieces = []
    for _ in range(ALIBI_SPLIT):
        p = x.astype(BF16).astype(F32)
        pieces.append(p)
        x = x - p
    return pieces


def _diff_attn_kernel(q_ref, k_ref, v_ref, pq_ref, pk_ref, c_ref, lam_ref, sub_ref, o_ref,
                      w_ref, s_ref, mt_ref, m_ref, l_ref, acc_ref, *, tq, tk, nq, lambda_init, sorted_pos,
                      band=None):
    dh = DIFF_HEAD_DIM
    n_rep = tq // LANES
    lane_tile = lambda v: jnp.concatenate([v] * n_rep, axis=1)

    if sorted_pos:
        w_ref[:, LANES + ALIBI_ROWS:, :] = jnp.zeros((4, LANES - ALIBI_ROWS, tq), BF16)

    def switch(qn):
        qt = q_ref[0, qn]
        row = lax.broadcasted_iota(jnp.int32, qt.shape, 0)
        for c in range(4):
            w_ref[c, 0:LANES, :] = jnp.where((row >= c * dh) & (row < (c + 1) * dh), qt, jnp.zeros_like(qt))
        if sorted_pos:
            relq = pq_ref[0, qn].astype(F32)
            for hh in range(2):
                cp = [lane_tile(c_ref[0, ALIBI_SPLIT * hh + i:ALIBI_SPLIT * hh + i + 1, :])
                      for i in range(ALIBI_SPLIT)]
                tp = _bf16_pieces(-(cp[0] + cp[1] + cp[2]) * relq)
                rows = ([float(1 << POS_DIGIT_BITS) * p for p in cp] + cp + tp
                        + [jnp.zeros((ALIBI_ROWS - 3 * ALIBI_SPLIT, tq), F32)])
                blk = jnp.concatenate(rows, axis=0).astype(BF16)
                for mp in range(2):
                    w_ref[2 * hh + mp, LANES:LANES + ALIBI_ROWS, :] = blk

    def prep(qn, kn):
        ks = pl.multiple_of(kn * tk, tk)
        if sorted_pos:
            return ks, None
        pq = pq_ref[0, qn]
        pk = pk_ref[0, pl.ds(ks, tk), :]
        dist = jnp.concatenate(
            [jnp.abs(pq[:, j * LANES:(j + 1) * LANES] - pk) for j in range(n_rep)], axis=1).astype(F32)
        return ks, [dist * lane_tile(c_ref[0, hh:hh + 1, :]) for hh in range(2)]

    def scores(ctx, c, cs):
        ks, bias = ctx
        if sorted_pos:
            lhs = jnp.concatenate([k_ref[0, pl.ds(ks, tk), :], pk_ref[0, pl.ds(ks, tk), :]], axis=1)
            return _dot(lhs, w_ref[c, :, cs])
        return _dot(k_ref[0, pl.ds(ks, tk), :], w_ref[c, :, cs]) + bias[c // 2][:, cs]

    def finish(qi):
        lam_v = lam_ref[...]
        lam = (jnp.exp(jnp.sum(lam_v[0:1] * lam_v[1:2], axis=1, keepdims=True))
               - jnp.exp(jnp.sum(lam_v[2:3] * lam_v[3:4], axis=1, keepdims=True)) + lambda_init)
        sub = lane_tile(sub_ref[...])
        outs = []
        for hh in range(2):
            c0, c1 = 2 * hh, 2 * hh + 1
            o = acc_ref[c0] / l_ref[c0] - lam * (acc_ref[c1] / l_ref[c1])
            o = o * lax.rsqrt(jnp.mean(o * o, axis=0, keepdims=True) + EPS) * sub
            outs.append(o * (1.0 - lambda_init))
        o_ref[0, _tile_rows(qi, tq), :] = jnp.concatenate(outs, axis=0).T.astype(BF16)

    kstart_fn = n_tiles = None
    if band is not None:
        kstart_ref, ntile_ref = band
        pair = pl.program_id(0) * pl.num_programs(1) + pl.program_id(1)
        kstart_fn = lambda qi: kstart_ref[pair * nq + qi]
        n_tiles = ntile_ref[pair]

    _flash_causal(nq, 4, tq, tk, switch, prep, scores,
                  lambda kj, c: v_ref[0, kj, (c // 2) * DIFF_V_DIM:(c // 2 + 1) * DIFF_V_DIM, :], finish,
                  s_ref, mt_ref, m_ref, l_ref, acc_ref, kstart_fn=kstart_fn, n_tiles=n_tiles)


def _diff_attn_band_kernel(kstart_ref, ntile_ref, *refs, **kw):
    _diff_attn_kernel(*refs, band=(kstart_ref, ntile_ref), **kw)


def _diff_attention_call(qd, kd, vd, pq, pk, cvec, lam_params, sub, lambda_init, sorted_pos, band=None):
    b, nq, _, tq = qd.shape
    nk, tk = vd.shape[1], vd.shape[3]
    s = nq * tq
    n_pair = DIFF_HEADS // 2
    w_rows = 2 * LANES if sorted_pos else LANES

    def spec(shape, imap):
        return pl.BlockSpec(shape, lambda bi, p, *_: imap(bi, p))

    grid_spec = pltpu.PrefetchScalarGridSpec(
        num_scalar_prefetch=0 if band is None else len(band),
        grid=(b, n_pair),
        in_specs=[
            spec((1, nq, LANES, tq), lambda bi, p: (bi, 0, p, 0)),
            spec((1, s, LANES), lambda bi, p: (bi, 0, p)),
            spec((1, nk, 2 * DIFF_V_DIM, tk), lambda bi, p: (bi, 0, p, 0)),
            spec((1, nq, 1, tq), lambda bi, p: (bi, 0, 0, 0)),
            spec((1, s, LANES), lambda bi, p: (bi, 0, 0)),
            spec((1,) + cvec.shape[1:], lambda bi, p: (p, 0, 0)),
            spec((4, DIFF_HEAD_DIM), lambda bi, p: (0, 0)),
            spec((DIFF_V_DIM, LANES), lambda bi, p: (0, 0)),
        ],
        out_specs=spec((1, s, 2 * DIFF_V_DIM), lambda bi, p: (bi, 0, p)),
        scratch_shapes=[pltpu.VMEM((4, w_rows, tq), BF16), pltpu.VMEM((4, tk, tq), F32),
                        pltpu.VMEM((4, 1, tq), F32), pltpu.VMEM((4, 1, tq), F32), pltpu.VMEM((4, 1, tq), F32),
                        pltpu.VMEM((4, DIFF_V_DIM, tq), F32)],
    )
    body = _diff_attn_kernel if band is None else _diff_attn_band_kernel
    return pl.pallas_call(
        functools.partial(body, tq=tq, tk=tk, nq=nq, lambda_init=lambda_init, sorted_pos=sorted_pos),
        grid_spec=grid_spec,
        out_shape=jax.ShapeDtypeStruct((b, s, DIFF_HEADS * DIFF_V_DIM), BF16),
        compiler_params=_cparams(("parallel", "parallel")),
        name="diff_attention_sorted" if sorted_pos else "diff_attention",
    )(*(band or ()), qd, kd, vd, pq, pk, cvec, lam_params, sub)


UNDERFLOW_LOG2 = 160.0


def _diff_band(qn2, kn2, rel, c, tq, tk):
    b = rel.shape[0]
    nq = rel.shape[1] // tq
    n_pair = DIFF_HEADS // 2
    qmax = jnp.max(qn2, axis=(1, 3)).reshape(b, DIFF_HEADS, 2)
    kmax = jnp.max(kn2, axis=(1, 2)).reshape(b, DIFF_HEADS, 2)
    bound = jnp.max(jnp.sqrt(qmax * kmax), axis=-1)
    width = (UNDERFLOW_LOG2 + 2.05 * bound + 1.0) / c.reshape(1, DIFF_HEADS)
    width = jnp.max(width.reshape(b, n_pair, 2), axis=-1)
    first_q = rel[:, ::tq]
    last_k = rel[:, tk - 1::tk]
    nk = last_k.shape[1]
    dmin = (first_q[:, :, None] - last_k[:, None, :]).astype(F32)
    r = tq // tk
    below = jnp.arange(nk)[None, :] < r * jnp.arange(nq)[:, None]
    skip = (dmin[:, None] >= width[:, :, None, None]) & below[None, None]
    kstart = jnp.sum(skip, axis=-1).astype(jnp.int32)
    ntile = jnp.sum(r * jnp.arange(1, nq + 1, dtype=jnp.int32)[None, None, :] - kstart, axis=-1)
    return kstart.reshape(-1), ntile.reshape(-1).astype(jnp.int32)


def _diff_attention(qd, kd, vd, qn2, kn2, positions, slopes, lam_params, subln, lambda_init):
    b, nq, _, t = qd.shape
    s = nq * t
    n_pair = DIFF_HEADS // 2
    c = (slopes * LOG2E).reshape(n_pair, 2, 1)
    sub = jnp.broadcast_to(subln[:, None], (DIFF_V_DIM, LANES))
    rel = positions - positions[:, :1]
    span = 1 << (2 * POS_DIGIT_BITS)
    sorted_pos = (jnp.all(positions[:, 1:] >= positions[:, :-1]) & jnp.all((rel >= 0) & (rel < span)))

    def sorted_path(_):
        hi = (rel >> POS_DIGIT_BITS).astype(BF16)
        lo = (rel & ((1 << POS_DIGIT_BITS) - 1)).astype(BF16)
        one = jnp.ones_like(hi)
        feats = jnp.stack([hi] * ALIBI_SPLIT + [lo] * ALIBI_SPLIT + [one] * ALIBI_SPLIT, axis=-1)
        feats = jnp.pad(feats, ((0, 0), (0, 0), (0, LANES - feats.shape[-1])))
        pieces = jnp.concatenate(_bf16_pieces(c), axis=-1).reshape(n_pair, 2 * ALIBI_SPLIT, 1)
        cvec = jnp.broadcast_to(jnp.pad(pieces, ((0, 0), (0, 8 - 2 * ALIBI_SPLIT), (0, 0))), (n_pair, 8, LANES))
        band = _diff_band(qn2, kn2, rel, c, t, vd.shape[3])
        return _diff_attention_call(qd, kd, vd, rel.reshape(b, nq, 1, t), feats, cvec, lam_params, sub,
                                    lambda_init, True, band)

    def general_path(_):
        pk = jnp.broadcast_to(positions[:, :, None], (b, s, LANES))
        cvec = jnp.broadcast_to(-c, (n_pair, 2, LANES))
        return _diff_attention_call(qd, kd, vd, positions.reshape(b, nq, 1, t), pk, cvec, lam_params, sub,
                                    lambda_init, False)

    return lax.cond(sorted_pos, sorted_path, general_path, None)


def _merge_kernel(h_ref, g_ref, ym_ref, yd_ref, pin_ref, halo_ref, wgate_ref, wb_ref, pw_ref, pb_ref, ps_ref,
                  wo_ref, o_ref, ext_ref, *, tm):
    i = pl.program_id(1)
    h = h_ref[0]
    d = h.shape[1]
    u = _rms(h, g_ref[...]).astype(BF16)

    x = pin_ref[0]
    ext_ref[0:POOL_HALO, :] = jnp.where(i == 0, jnp.zeros_like(halo_ref[0]), halo_ref[0])
    ext_ref[POOL_HALO:, :] = x
    tpos = i * tm + lax.broadcasted_iota(jnp.int32, (tm, POOL_GROUP), 0)
    yp = []
    for gi, w in enumerate(POOL_WINDOWS):
        ls = slice(gi * POOL_GROUP, (gi + 1) * POOL_GROUP)
        tot = ext_ref[POOL_HALO:POOL_HALO + tm, ls]
        for j in range(1, w):
            tot = tot + ext_ref[POOL_HALO - j:POOL_HALO - j + tm, ls]
        cnt = jnp.minimum(tpos + 1, w).astype(F32)
        pooled = tot / cnt - x[:, ls]
        yp.append((_dot(pooled.astype(BF16), pw_ref[gi]) + pb_ref[gi:gi + 1, :]) * ps_ref[:, ls])
    y_pool = jnp.concatenate(yp, axis=1).astype(BF16)

    merged = jnp.zeros((tm, d), F32)
    for bi, y in enumerate((ym_ref[0], yd_ref[0], y_pool)):
        gate = jax.nn.sigmoid(_dot(u, wgate_ref[:, bi * d:(bi + 1) * d]))
        merged = merged + gate * _dot(y, wb_ref[bi])
    o_ref[0] = h + _dot(merged.astype(BF16), wo_ref[...])


def _merge(h, g, y_mla, y_diff, pin, wts):
    b, s, d = h.shape
    tm = min(MERGE_TM, s)
    assert s % tm == 0 and tm % POOL_HALO == 0
    hb = tm // POOL_HALO
    consts = [wts[n] for n in ("wgate", "wb", "pw", "pb", "ps", "wo")]

    def cspec(a):
        return pl.BlockSpec(a.shape, lambda bi, i, _n=a.ndim: (0,) * _n)

    tile = lambda w: pl.BlockSpec((1, tm, w), lambda bi, i: (bi, i, 0))
    return pl.pallas_call(
        functools.partial(_merge_kernel, tm=tm),
        grid=(b, s // tm),
        in_specs=[tile(d), pl.BlockSpec((1, d), lambda bi, i: (0, 0)), tile(BRANCH_WIDTH), tile(BRANCH_WIDTH),
                  tile(512),
                  pl.BlockSpec((1, POOL_HALO, 512), lambda bi, i: (bi, jnp.maximum(i * hb - 1, 0), 0))]
                 + [cspec(a) for a in consts],
        out_specs=tile(d),
        out_shape=jax.ShapeDtypeStruct((b, s, d), F32),
        scratch_shapes=[pltpu.VMEM((tm + POOL_HALO, 512), F32)],
        compiler_params=_cparams(("parallel", "parallel")),
        name="gated_merge",
    )(h, g.reshape(1, d), y_mla, y_diff, pin, pin, *consts)


def _rot_half_cols(w):
    half = w.shape[-1] // 2
    return jnp.concatenate([-w[..., half:], w[..., :half]], axis=-1)


def _layer_weights(l, w_in, mla_q_norm, mla_w_uq, mla_kv_norm, mla_w_ukv, pool_w, pool_b, pool_scale,
                   w_branch, w_out):
    d = w_in.shape[1]
    win = w_in[l]
    o_kv = MLA_Q_RANK
    o_kr = o_kv + MLA_KV_RANK
    o_dq = o_kr + MLA_ROPE_DIM
    o_dk = o_dq + 512
    o_dv = o_dk + 512
    o_p = o_dv + 512
    o_g = o_p + 512
    w_kr = win[:, o_kr:o_dq]
    wn = jnp.concatenate([win[:, :o_dq], _rot_half_cols(w_kr), jnp.zeros((d, 64), F32),
                          win[:, o_dk:o_dv], win[:, o_p:o_g]], axis=1)
    wt = jnp.concatenate([win[:, o_dq:o_dk], win[:, o_dv:o_p]], axis=1).T

    wq = mla_w_uq[l].reshape(MLA_Q_RANK, MLA_HEADS, MLA_NOPE_DIM + MLA_ROPE_DIM)
    rope = wq[..., MLA_NOPE_DIM:]
    pad = jnp.zeros((MLA_Q_RANK, MLA_HEADS, MLA_HEAD_PAD - MLA_NOPE_DIM - MLA_ROPE_DIM), F32)
    w1 = jnp.concatenate([wq, pad], axis=-1).reshape(MLA_Q_RANK, MLA_HEADS * MLA_HEAD_PAD).T
    w2 = _rot_half_cols(rope).reshape(MLA_Q_RANK, MLA_HEADS * MLA_ROPE_DIM).T

    wkv = mla_w_ukv[l]
    lane = jnp.arange(wkv.shape[1]) % (MLA_NOPE_DIM + MLA_V_DIM)
    wkn = jnp.where(lane[None, :] < MLA_NOPE_DIM, wkv, 0.0)
    wv = wkv.reshape(MLA_KV_RANK, MLA_HEADS, MLA_NOPE_DIM + MLA_V_DIM)[..., MLA_NOPE_DIM:]
    wv = wv.reshape(MLA_KV_RANK, MLA_HEADS * MLA_V_DIM).T
    col = jnp.arange(MLA_HEADS * MLA_HEAD_PAD)
    e = ((col[None, :] % MLA_HEAD_PAD) == (MLA_NOPE_DIM + jnp.arange(MLA_ROPE_DIM)[:, None])).astype(BF16)

    bf = lambda a: a.astype(BF16)
    return {
        "wn": bf(wn), "wt": bf(wt), "qn": mla_q_norm[l].reshape(1, -1), "kvn": mla_kv_norm[l].reshape(1, -1),
        "w1": bf(w1), "w2": bf(w2), "wkn": bf(wkn), "e": e, "wv": bf(wv),
        "wgate": bf(win[:, o_g:]), "wb": bf(w_branch[l]), "pw": bf(pool_w[l]), "pb": pool_b[l],
        "ps": pool_scale[l].reshape(1, -1), "wo": bf(w_out[l]),
    }


def kernel(x, positions, ffn1_norm, ffn1_w_gate, ffn1_w_up, ffn1_w_down, mix_norm, w_in, mla_q_norm, mla_w_uq, mla_kv_norm, mla_w_ukv, diff_lambda_q1, diff_lambda_k1, diff_lambda_q2, diff_lambda_k2, diff_subln, pool_w, pool_b, pool_scale, w_branch, w_out, ffn2_norm, ffn2_w_gate, ffn2_w_up, ffn2_w_down, final_norm):
    b, s, d = x.shape
    depth = w_in.shape[0]
    tq, tk = min(ATT_TQ, s), min(ATT_TK, s)
    assert s % tq == 0

    half = MLA_ROPE_DIM // 2
    inv_freq = ROPE_BASE ** (-jnp.arange(half, dtype=F32) / half)
    tabs = _rope_tables(positions, jnp.concatenate([inv_freq, inv_freq]))
    slopes = jnp.exp2(-8.0 * jnp.arange(1, DIFF_HEADS + 1, dtype=F32) / DIFF_HEADS)

    bf = lambda a: a.astype(BF16)
    ffn1_w = (bf(ffn1_w_gate), bf(ffn1_w_up), bf(ffn1_w_down))
    ffn2_w = (bf(ffn2_w_gate), bf(ffn2_w_up), bf(ffn2_w_down))
    h = x.reshape(b * s, d)
    for l in range(depth):
        h = _ffn(h, ffn1_norm[l], *ffn1_w, l)
        wts = _layer_weights(l, w_in, mla_q_norm, mla_w_uq, mla_kv_norm, mla_w_ukv, pool_w, pool_b, pool_scale,
                             w_branch, w_out)
        h3 = h.reshape(b, s, d)
        qm, km, vm, qd, kd, vd, pin, qn2, kn2 = _proj(h3, mix_norm[l], wts, tabs, tq, tk)
        y_mla = _mla_attention(qm, km, vm)
        lambda_init = 0.8 - 0.6 * math.exp(-0.3 * l)
        lam_params = jnp.stack([diff_lambda_q1[l], diff_lambda_k1[l], diff_lambda_q2[l], diff_lambda_k2[l]])
        y_diff = _diff_attention(qd, kd, vd, qn2, kn2, positions, slopes, lam_params, diff_subln[l], lambda_init)
        h = _merge(h3, mix_norm[l], y_mla, y_diff, pin, wts).reshape(b * s, d)
        last = l == depth - 1
        h = _ffn(h, ffn2_norm[l], *ffn2_w, l, final_g=final_norm if last else None)
    return h.reshape(b, s, d)
```

```python
import functools
import math

import jax
import jax.numpy as jnp
from jax import lax
from jax.experimental import pallas as pl
from jax.experimental.pallas import tpu as pltpu

F32 = jnp.float32
BF16 = jnp.bfloat16

EPS = 1e-6
NEG_INF = -1e30
LOG2E = math.log2(math.e)

MLA_HEADS = 8
MLA_NOPE_DIM = 64
MLA_ROPE_DIM = 32
MLA_V_DIM = 64
MLA_Q_RANK = 256
MLA_KV_RANK = 128
ROPE_BASE = 10000.0
MLA_HEAD_PAD = 128
MLA_GROUP = 4

DIFF_HEADS = 8
DIFF_HEAD_DIM = 32
DIFF_V_DIM = 2 * DIFF_HEAD_DIM

POOL_WINDOWS = (2, 4, 8, 16)
POOL_GROUP = 128
POOL_HALO = 16
N_BRANCH = 3
BRANCH_WIDTH = 512

LANES = 128
VMEM_LIMIT_BYTES = 56 * 1024 * 1024

FFN_TM = 2048
FFN_TF = 256
ATT_TQ = 512
ATT_TK = 512
ATT_COL_SPLIT = 1
MERGE_TM = 1024
TAB_TM = 1024


def _cparams(sem):
    return pltpu.CompilerParams(dimension_semantics=sem, vmem_limit_bytes=VMEM_LIMIT_BYTES)


def _rms(x, g):
    return x * lax.rsqrt(jnp.mean(x * x, axis=-1, keepdims=True) + EPS) * g


def _dot(a, b):
    return jnp.dot(a, b, preferred_element_type=F32)


def _dot_nt(a, b):
    return lax.dot_general(a, b, (((1,), (1,)), ((), ())), preferred_element_type=F32)


def _ffn_kernel(h_ref, g_ref, wg_ref, wu_ref, wd_ref, *rest, n_f, final):
    if final:
        fg_ref, o_ref, xn_ref, acc_ref = rest
    else:
        o_ref, xn_ref, acc_ref = rest
    f = pl.program_id(1)

    @pl.when(f == 0)
    def _():
        xn_ref[...] = _rms(h_ref[...], g_ref[...]).astype(BF16)
        acc_ref[...] = jnp.zeros_like(acc_ref)

    xn = xn_ref[...]
    a = _dot(xn, wg_ref[...])
    b = _dot(xn, wu_ref[...])
    hid = (a * jax.nn.sigmoid(a)) * b
    acc_ref[...] += _dot(hid.astype(BF16), wd_ref[...])

    @pl.when(f == n_f - 1)
    def _():
        y = h_ref[...] + 0.5 * acc_ref[...]
        if final:
            y = _rms(y, fg_ref[...])
        o_ref[...] = y


def _ffn(h, g, wg, wu, wd, l, final_g=None):
    t, d = h.shape
    ff = wg.shape[2]
    tm, tf = min(FFN_TM, t), FFN_TF
    assert t % tm == 0 and ff % tf == 0
    n_f = ff // tf
    final = final_g is not None
    in_specs = [
        pl.BlockSpec((tm, d), lambda i, f: (i, 0)),
        pl.BlockSpec((1, d), lambda i, f: (0, 0)),
        pl.BlockSpec((None, d, tf), lambda i, f: (l, 0, f)),
        pl.BlockSpec((None, d, tf), lambda i, f: (l, 0, f)),
        pl.BlockSpec((None, tf, d), lambda i, f: (l, f, 0)),
    ]
    args = [h, g.reshape(1, d), wg, wu, wd]
    if final:
        in_specs.append(pl.BlockSpec((1, d), lambda i, f: (0, 0)))
        args.append(final_g.reshape(1, d))
    return pl.pallas_call(
        functools.partial(_ffn_kernel, n_f=n_f, final=final),
        grid=(t // tm, n_f),
        in_specs=in_specs,
        out_specs=pl.BlockSpec((tm, d), lambda i, f: (i, 0)),
        out_shape=jax.ShapeDtypeStruct((t, d), F32),
        scratch_shapes=[pltpu.VMEM((tm, d), BF16), pltpu.VMEM((tm, d), F32)],
        compiler_params=_cparams(("parallel", "arbitrary")),
        name="ffn",
    )(*args)


def _rope_table_kernel(posc_ref, posr_ref, fr_ref, fc_ref, cn_ref, sn_ref, ct_ref, st_ref):
    ang_n = posc_ref[0].astype(F32) * fr_ref[...]
    cn_ref[0] = jnp.cos(ang_n)
    sn_ref[0] = jnp.sin(ang_n)
    ang_t = fc_ref[...] * posr_ref[0].astype(F32)
    ct_ref[0] = jnp.cos(ang_t)
    st_ref[0] = jnp.sin(ang_t)


def _rope_tables(positions, inv_freq2):
    b, s = positions.shape
    r = inv_freq2.shape[0]
    tm = min(TAB_TM, s)
    assert s % tm == 0
    nat = jax.ShapeDtypeStruct((b, s, r), F32)
    tr = jax.ShapeDtypeStruct((b, r, s), F32)
    return pl.pallas_call(
        _rope_table_kernel,
        grid=(b, s // tm),
        in_specs=[
            pl.BlockSpec((1, tm, 1), lambda bi, i: (bi, i, 0)),
            pl.BlockSpec((1, 1, tm), lambda bi, i: (bi, 0, i)),
            pl.BlockSpec((1, r), lambda bi, i: (0, 0)),
            pl.BlockSpec((r, 1), lambda bi, i: (0, 0)),
        ],
        out_specs=[
            pl.BlockSpec((1, tm, r), lambda bi, i: (bi, i, 0)),
            pl.BlockSpec((1, tm, r), lambda bi, i: (bi, i, 0)),
            pl.BlockSpec((1, r, tm), lambda bi, i: (bi, 0, i)),
            pl.BlockSpec((1, r, tm), lambda bi, i: (bi, 0, i)),
        ],
        out_shape=[nat, nat, tr, tr],
        compiler_params=_cparams(("parallel", "parallel")),
        name="rope_tables",
    )(positions.reshape(b, s, 1), positions.reshape(b, 1, s), inv_freq2.reshape(1, r), inv_freq2.reshape(r, 1))


def _proj_kernel(h_ref, g_ref, wn_ref, wt_ref, qn_ref, kvn_ref, w1_ref, w2_ref, wkn_ref, e_ref, wv_ref,
                 cn_ref, sn_ref, ct_ref, st_ref,
                 grp_ref, qm_ref, km_ref, vm_ref, qd_ref, kd_ref, vd_ref, pin_ref, qn2_ref, kn2_ref):
    r = MLA_ROPE_DIM
    u = _rms(h_ref[0], g_ref[...]).astype(BF16)
    zn = _dot(u, wn_ref[...])
    zt = _dot_nt(wt_ref[...], u)

    cqn = _rms(zn[:, :MLA_Q_RANK], qn_ref[...]).astype(BF16)
    ckvn = _rms(zn[:, MLA_Q_RANK:MLA_Q_RANK + MLA_KV_RANK], kvn_ref[...]).astype(BF16)
    o = MLA_Q_RANK + MLA_KV_RANK
    kr, krp = zn[:, o:o + r], zn[:, o + r:o + 2 * r]

    qscale = (MLA_NOPE_DIM + MLA_ROPE_DIM) ** -0.5 * LOG2E
    q1 = _dot_nt(w1_ref[...], cqn)
    q2 = _dot_nt(w2_ref[...], cqn)
    ct, st = ct_ref[0], st_ref[0]
    for hd in range(MLA_HEADS):
        base = hd * MLA_HEAD_PAD
        qm_ref[0, 0, base:base + MLA_NOPE_DIM, :] = (q1[base:base + MLA_NOPE_DIM] * qscale).astype(BF16)
        ro = base + MLA_NOPE_DIM
        roped = q1[ro:ro + r] * ct + q2[hd * r:(hd + 1) * r] * st
        qm_ref[0, 0, ro:ro + r, :] = (roped * qscale).astype(BF16)
        qm_ref[0, 0, ro + r:base + MLA_HEAD_PAD, :] = jnp.zeros((MLA_HEAD_PAD - MLA_NOPE_DIM - r, q1.shape[1]), BF16)

    kro = (kr * cn_ref[0] + krp * sn_ref[0]).astype(BF16)
    km_ref[0] = (_dot(ckvn, wkn_ref[...]) + _dot(kro, e_ref[...])).astype(BF16)

    vmt = _dot_nt(wv_ref[...], ckvn).astype(BF16)
    qdt = (zt[:DIFF_HEADS * 2 * DIFF_HEAD_DIM] * (DIFF_HEAD_DIM ** -0.5 * LOG2E)).astype(BF16)
    vdt = zt[DIFF_HEADS * 2 * DIFF_HEAD_DIM:].astype(BF16)
    vm_ref[0, 0] = vmt
    vd_ref[0, 0] = vdt
    qd_ref[0, 0] = qdt
    kdn = zn[:, 512:1024].astype(BF16)
    kd_ref[0] = kdn
    pin_ref[0] = zn[:, 1024:1536]

    qf, kf = qdt.astype(F32), kdn.astype(F32)
    qsum = _dot(grp_ref[...], (qf * qf).astype(BF16))
    ksum = _dot_nt((kf * kf).astype(BF16), grp_ref[...])
    n_grp = grp_ref.shape[0]
    qn2_ref[0, 0] = jnp.broadcast_to(jnp.max(qsum, axis=1, keepdims=True), (n_grp, LANES))
    kn2_ref[0, 0] = jnp.broadcast_to(jnp.max(ksum, axis=0, keepdims=True), (8, n_grp))


def _proj(h, g, wts, tabs, tq, tk):
    b, s, d = h.shape
    tm = tk
    assert s % tq == 0 and tq % tm == 0
    qr = tq // tm
    cn, sn, ct, st = tabs
    names = ("wn", "wt", "qn", "kvn", "w1", "w2", "wkn", "e", "wv")
    consts = [wts[n] for n in names]

    def cspec(a):
        return pl.BlockSpec(a.shape, lambda bi, i, _n=a.ndim: (0,) * _n)

    hm, hd_ = MLA_HEADS * MLA_HEAD_PAD, DIFF_HEADS * 2 * DIFF_HEAD_DIM
    vw = MLA_HEADS * MLA_V_DIM
    n_grp = DIFF_HEADS * 2
    grp = (jnp.arange(hd_)[None, :] // DIFF_HEAD_DIM == jnp.arange(n_grp)[:, None]).astype(BF16)
    out_shape = [
        jax.ShapeDtypeStruct((b, s // tq, hm, tq), BF16),
        jax.ShapeDtypeStruct((b, s, hm), BF16),
        jax.ShapeDtypeStruct((b, s // tk, vw, tk), BF16),
        jax.ShapeDtypeStruct((b, s // tq, hd_, tq), BF16),
        jax.ShapeDtypeStruct((b, s, hd_), BF16),
        jax.ShapeDtypeStruct((b, s // tk, hd_, tk), BF16),
        jax.ShapeDtypeStruct((b, s, 512), F32),
        jax.ShapeDtypeStruct((b, s // tm, n_grp, LANES), F32),
        jax.ShapeDtypeStruct((b, s // tm, 8, n_grp), F32),
    ]
    qmaj = lambda rows: pl.BlockSpec((1, 1, rows, tm), lambda bi, i: (bi, i // qr, 0, i % qr))
    vmaj = lambda rows: pl.BlockSpec((1, 1, rows, tk), lambda bi, i: (bi, i, 0, 0))
    out_specs = [
        qmaj(hm),
        pl.BlockSpec((1, tm, hm), lambda bi, i: (bi, i, 0)),
        vmaj(vw),
        qmaj(hd_),
        pl.BlockSpec((1, tm, hd_), lambda bi, i: (bi, i, 0)),
        vmaj(hd_),
        pl.BlockSpec((1, tm, 512), lambda bi, i: (bi, i, 0)),
        pl.BlockSpec((1, 1, n_grp, LANES), lambda bi, i: (bi, i, 0, 0)),
        pl.BlockSpec((1, 1, 8, n_grp), lambda bi, i: (bi, i, 0, 0)),
    ]
    r = MLA_ROPE_DIM
    in_specs = ([pl.BlockSpec((1, tm, d), lambda bi, i: (bi, i, 0)), pl.BlockSpec((1, d), lambda bi, i: (0, 0))]
                + [cspec(a) for a in consts]
                + [pl.BlockSpec((1, tm, r), lambda bi, i: (bi, i, 0))] * 2
                + [pl.BlockSpec((1, r, tm), lambda bi, i: (bi, 0, i))] * 2
                + [cspec(grp)])
    return pl.pallas_call(
        _proj_kernel,
        grid=(b, s // tm),
        in_specs=in_specs,
        out_specs=out_specs,
        out_shape=out_shape,
        compiler_params=_cparams(("parallel", "parallel")),
        name="mixer_proj",
    )(h, g.reshape(1, d), *consts, cn, sn, ct, st, grp)


ONES_ROWS = 16


def _tile_rows(i, t):
    return pl.ds(i * t, t) if isinstance(i, int) else pl.ds(pl.multiple_of(i * t, t), t)


def _flash_causal(nq, n_chains, tq, tk, switch_fn, prep_fn, score_fn, value_fn, finish_fn,
                  s_ref, mt_ref, m_ref, l_ref, acc_ref, kstart_fn=None, n_tiles=None):
    dv = acc_ref.shape[1]
    r = tq // tk
    ones = jnp.ones((ONES_ROWS, tk), BF16)
    cw = tq // ATT_COL_SPLIT
    items = [(c, slice(h * cw, (h + 1) * cw)) for c in range(n_chains) for h in range(ATT_COL_SPLIT)]

    def masked_scores(ctx, c, cs, j):
        row = lax.broadcasted_iota(jnp.int32, (tk, cw), 0)
        col = lax.broadcasted_iota(jnp.int32, (tk, cw), 1) + cs.start
        return jnp.where(row + j * tk <= col, score_fn(ctx, c, cs), NEG_INF)

    m_ref[...] = jnp.full(m_ref.shape, NEG_INF, F32)
    l_ref[...] = jnp.zeros_like(l_ref)
    acc_ref[...] = jnp.zeros_like(acc_ref)

    def put(c, cs, s):
        s_ref[c, :, cs] = s
        mt_ref[c, :, cs] = jnp.max(s, axis=0, keepdims=True)

    def first_k(qi):
        return 0 if kstart_fn is None else kstart_fn(qi)

    def consume(kj, first, c, cs):
        m_old = jnp.where(first, NEG_INF, m_ref[c, :, cs])
        m_new = jnp.maximum(m_old, mt_ref[c, :, cs])
        alpha = jnp.exp2(m_old - m_new)
        p = jnp.exp2(s_ref[c, :, cs] - m_new).astype(BF16)
        r = _dot(jnp.concatenate([value_fn(kj, c), ones], axis=0), p)
        acc_ref[c, :, cs] = alpha * acc_ref[c, :, cs] + r[:dv]
        l_ref[c, :, cs] = alpha * l_ref[c, :, cs] + r[dv:dv + 1]
        m_ref[c, :, cs] = m_new

    def step(kj, first, qn, kn, diag_next):
        ctx = prep_fn(qn, kn)
        for c, cs in items:
            s_next = score_fn(ctx, c, cs) if diag_next is None else masked_scores(ctx, c, cs, diag_next)
            consume(kj, first, c, cs)
            put(c, cs, s_next)

    if switch_fn is not None:
        switch_fn(0)
    ctx0 = prep_fn(0, 0)
    for c, cs in items:
        put(c, cs, masked_scores(ctx0, c, cs, 0))

    def body(_, carry):
        qi, kj = carry
        first = kj == first_k(qi)
        last = kj == r * qi + (r - 1)
        qn = jnp.where(last, qi + 1, qi)
        kn = jnp.where(last, first_k(jnp.minimum(qi + 1, nq - 1)), kj + 1)

        if switch_fn is not None:
            pl.when(last)(lambda: switch_fn(qn))

        for j in range(r):
            pl.when(kn == r * qn + j)(functools.partial(step, kj, first, qn, kn, j))
        pl.when(kn < r * qn)(functools.partial(step, kj, first, qn, kn, None))

        @pl.when(last)
        def _():
            finish_fn(qi)

        return qn, kn

    if n_tiles is None:
        n_tiles = r * nq * (nq + 1) // 2
    lax.fori_loop(0, n_tiles - 1, body, (jnp.int32(0), jnp.int32(0)))
    k_end = jnp.int32(r * nq - 1)
    for c, cs in items:
        consume(k_end, k_end == first_k(nq - 1), c, cs)
    finish_fn(nq - 1)


def _mla_attn_kernel(q_ref, k_ref, v_ref, o_ref, s_ref, mt_ref, m_ref, l_ref, acc_ref, *, tq, tk, nq):
    hp = MLA_HEAD_PAD

    def prep(qn, kn):
        return qn, pl.multiple_of(kn * tk, tk)

    def scores(ctx, c, cs):
        qn, ks = ctx
        return _dot(k_ref[0, pl.ds(ks, tk), c * hp:(c + 1) * hp], q_ref[0, qn, c * hp:(c + 1) * hp, cs])

    def finish(qi):
        o = jnp.concatenate([acc_ref[c] / l_ref[c] for c in range(MLA_GROUP)], axis=0)
        o_ref[0, _tile_rows(qi, tq), :] = o.T.astype(BF16)

    _flash_causal(nq, MLA_GROUP, tq, tk, None, prep, scores,
                  lambda kj, c: v_ref[0, kj, c * MLA_V_DIM:(c + 1) * MLA_V_DIM, :], finish,
                  s_ref, mt_ref, m_ref, l_ref, acc_ref)


def _mla_attention(qm, km, vm):
    b, nq, _, tq = qm.shape
    nk, tk = vm.shape[1], vm.shape[3]
    s = nq * tq
    g = MLA_GROUP
    hpg = g * MLA_HEAD_PAD
    return pl.pallas_call(
        functools.partial(_mla_attn_kernel, tq=tq, tk=tk, nq=nq),
        grid=(b, MLA_HEADS // g),
        in_specs=[
            pl.BlockSpec((1, nq, hpg, tq), lambda bi, p: (bi, 0, p, 0)),
            pl.BlockSpec((1, s, hpg), lambda bi, p: (bi, 0, p)),
            pl.BlockSpec((1, nk, g * MLA_V_DIM, tk), lambda bi, p: (bi, 0, p, 0)),
        ],
        out_specs=pl.BlockSpec((1, s, g * MLA_V_DIM), lambda bi, p: (bi, 0, p)),
        out_shape=jax.ShapeDtypeStruct((b, s, MLA_HEADS * MLA_V_DIM), BF16),
        scratch_shapes=[pltpu.VMEM((g, tk, tq), F32), pltpu.VMEM((g, 1, tq), F32), pltpu.VMEM((g, 1, tq), F32),
                        pltpu.VMEM((g, 1, tq), F32), pltpu.VMEM((g, MLA_V_DIM, tq), F32)],
        compiler_params=_cparams(("parallel", "parallel")),
        name="mla_attention",
    )(qm, km, vm)


ALIBI_ROWS = 16
ALIBI_SPLIT = 3
POS_DIGIT_BITS = 8


def _bf16_pieces(x):
    pieces = []
    for _ in range(ALIBI_SPLIT):
        p = x.astype(BF16).astype(F32)
        pieces.append(p)
        x = x - p
    return pieces


def _diff_attn_kernel(q_ref, k_ref, v_ref, pq_ref, pk_ref, c_ref, lam_ref, sub_ref, o_ref,
                      w_ref, s_ref, mt_ref, m_ref, l_ref, acc_ref, *, tq, tk, nq, lambda_init, sorted_pos,
                      band=None):
    dh = DIFF_HEAD_DIM
    n_rep = tq // LANES
    lane_tile = lambda v: jnp.concatenate([v] * n_rep, axis=1)

    if sorted_pos:
        w_ref[:, LANES + ALIBI_ROWS:, :] = jnp.zeros((4, LANES - ALIBI_ROWS, tq), BF16)

    def switch(qn):
        qt = q_ref[0, qn]
        row = lax.broadcasted_iota(jnp.int32, qt.shape, 0)
        for c in range(4):
            w_ref[c, 0:LANES, :] = jnp.where((row >= c * dh) & (row < (c + 1) * dh), qt, jnp.zeros_like(qt))
        if sorted_pos:
            relq = pq_ref[0, qn].astype(F32)
            for hh in range(2):
                cp = [lane_tile(c_ref[0, ALIBI_SPLIT * hh + i:ALIBI_SPLIT * hh + i + 1, :])
                      for i in range(ALIBI_SPLIT)]
                tp = _bf16_pieces(-(cp[0] + cp[1] + cp[2]) * relq)
                rows = ([float(1 << POS_DIGIT_BITS) * p for p in cp] + cp + tp
                        + [jnp.zeros((ALIBI_ROWS - 3 * ALIBI_SPLIT, tq), F32)])
                blk = jnp.concatenate(rows, axis=0).astype(BF16)
                for mp in range(2):
                    w_ref[2 * hh + mp, LANES:LANES + ALIBI_ROWS, :] = blk

    def prep(qn, kn):
        ks = pl.multiple_of(kn * tk, tk)
        if sorted_pos:
            return ks, None
        pq = pq_ref[0, qn]
        pk = pk_ref[0, pl.ds(ks, tk), :]
        dist = jnp.concatenate(
            [jnp.abs(pq[:, j * LANES:(j + 1) * LANES] - pk) for j in range(n_rep)], axis=1).astype(F32)
        return ks, [dist * lane_tile(c_ref[0, hh:hh + 1, :]) for hh in range(2)]

    def scores(ctx, c, cs):
        ks, bias = ctx
        if sorted_pos:
            lhs = jnp.concatenate([k_ref[0, pl.ds(ks, tk), :], pk_ref[0, pl.ds(ks, tk), :]], axis=1)
            return _dot(lhs, w_ref[c, :, cs])
        return _dot(k_ref[0, pl.ds(ks, tk), :], w_ref[c, :, cs]) + bias[c // 2][:, cs]

    def finish(qi):
        lam_v = lam_ref[...]
        lam = (jnp.exp(jnp.sum(lam_v[0:1] * lam_v[1:2], axis=1, keepdims=True))
               - jnp.exp(jnp.sum(lam_v[2:3] * lam_v[3:4], axis=1, keepdims=True)) + lambda_init)
        sub = lane_tile(sub_ref[...])
        outs = []
        for hh in range(2):
            c0, c1 = 2 * hh, 2 * hh + 1
            o = acc_ref[c0] / l_ref[c0] - lam * (acc_ref[c1] / l_ref[c1])
            o = o * lax.rsqrt(jnp.mean(o * o, axis=0, keepdims=True) + EPS) * sub
            outs.append(o * (1.0 - lambda_init))
        o_ref[0, _tile_rows(qi, tq), :] = jnp.concatenate(outs, axis=0).T.astype(BF16)

    kstart_fn = n_tiles = None
    if band is not None:
        kstart_ref, ntile_ref = band
        pair = pl.program_id(0) * pl.num_programs(1) + pl.program_id(1)
        kstart_fn = lambda qi: kstart_ref[pair * nq + qi]
        n_tiles = ntile_ref[pair]

    _flash_causal(nq, 4, tq, tk, switch, prep, scores,
                  lambda kj, c: v_ref[0, kj, (c // 2) * DIFF_V_DIM:(c // 2 + 1) * DIFF_V_DIM, :], finish,
                  s_ref, mt_ref, m_ref, l_ref, acc_ref, kstart_fn=kstart_fn, n_tiles=n_tiles)


def _diff_attn_band_kernel(kstart_ref, ntile_ref, *refs, **kw):
    _diff_attn_kernel(*refs, band=(kstart_ref, ntile_ref), **kw)


def _diff_attention_call(qd, kd, vd, pq, pk, cvec, lam_params, sub, lambda_init, sorted_pos, band=None):
    b, nq, _, tq = qd.shape
    nk, tk = vd.shape[1], vd.shape[3]
    s = nq * tq
    n_pair = DIFF_HEADS // 2
    w_rows = 2 * LANES if sorted_pos else LANES

    def spec(shape, imap):
        return pl.BlockSpec(shape, lambda bi, p, *_: imap(bi, p))

    grid_spec = pltpu.PrefetchScalarGridSpec(
        num_scalar_prefetch=0 if band is None else len(band),
        grid=(b, n_pair),
        in_specs=[
            spec((1, nq, LANES, tq), lambda bi, p: (bi, 0, p, 0)),
            spec((1, s, LANES), lambda bi, p: (bi, 0, p)),
            spec((1, nk, 2 * DIFF_V_DIM, tk), lambda bi, p: (bi, 0, p, 0)),
            spec((1, nq, 1, tq), lambda bi, p: (bi, 0, 0, 0)),
            spec((1, s, LANES), lambda bi, p: (bi, 0, 0)),
            spec((1,) + cvec.shape[1:], lambda bi, p: (p, 0, 0)),
            spec((4, DIFF_HEAD_DIM), lambda bi, p: (0, 0)),
            spec((DIFF_V_DIM, LANES), lambda bi, p: (0, 0)),
        ],
        out_specs=spec((1, s, 2 * DIFF_V_DIM), lambda bi, p: (bi, 0, p)),
        scratch_shapes=[pltpu.VMEM((4, w_rows, tq), BF16), pltpu.VMEM((4, tk, tq), F32),
                        pltpu.VMEM((4, 1, tq), F32), pltpu.VMEM((4, 1, tq), F32), pltpu.VMEM((4, 1, tq), F32),
                        pltpu.VMEM((4, DIFF_V_DIM, tq), F32)],
    )
    body = _diff_attn_kernel if band is None else _diff_attn_band_kernel
    return pl.pallas_call(
        functools.partial(body, tq=tq, tk=tk, nq=nq, lambda_init=lambda_init, sorted_pos=sorted_pos),
        grid_spec=grid_spec,
        out_shape=jax.ShapeDtypeStruct((b, s, DIFF_HEADS * DIFF_V_DIM), BF16),
        compiler_params=_cparams(("parallel", "parallel")),
        name="diff_attention_sorted" if sorted_pos else "diff_attention",
    )(*(band or ()), qd, kd, vd, pq, pk, cvec, lam_params, sub)


UNDERFLOW_LOG2 = 160.0


def _diff_band(qn2, kn2, rel, c, tq, tk):
    b = rel.shape[0]
    nq = rel.shape[1] // tq
    n_pair = DIFF_HEADS // 2
    qmax = jnp.max(qn2, axis=(1, 3)).reshape(b, DIFF_HEADS, 2)
    kmax = jnp.max(kn2, axis=(1, 2)).reshape(b, DIFF_HEADS, 2)
    bound = jnp.max(jnp.sqrt(qmax * kmax), axis=-1)
    width = (UNDERFLOW_LOG2 + 2.05 * bound + 1.0) / c.reshape(1, DIFF_HEADS)
    width = jnp.max(width.reshape(b, n_pair, 2), axis=-1)
    first_q = rel[:, ::tq]
    last_k = rel[:, tk - 1::tk]
    nk = last_k.shape[1]
    dmin = (first_q[:, :, None] - last_k[:, None, :]).astype(F32)
    r = tq // tk
    below = jnp.arange(nk)[None, :] < r * jnp.arange(nq)[:, None]
    skip = (dmin[:, None] >= width[:, :, None, None]) & below[None, None]
    kstart = jnp.sum(skip, axis=-1).astype(jnp.int32)
    ntile = jnp.sum(r * jnp.arange(1, nq + 1, dtype=jnp.int32)[None, None, :] - kstart, axis=-1)
    return kstart.reshape(-1), ntile.reshape(-1).astype(jnp.int32)


def _diff_attention(qd, kd, vd, qn2, kn2, positions, slopes, lam_params, subln, lambda_init):
    b, nq, _, t = qd.shape
    s = nq * t
    n_pair = DIFF_HEADS // 2
    c = (slopes * LOG2E).reshape(n_pair, 2, 1)
    sub = jnp.broadcast_to(subln[:, None], (DIFF_V_DIM, LANES))
    rel = positions - positions[:, :1]
    span = 1 << (2 * POS_DIGIT_BITS)
    sorted_pos = (jnp.all(positions[:, 1:] >= positions[:, :-1]) & jnp.all((rel >= 0) & (rel < span)))

    def sorted_path(_):
        hi = (rel >> POS_DIGIT_BITS).astype(BF16)
        lo = (rel & ((1 << POS_DIGIT_BITS) - 1)).astype(BF16)
        one = jnp.ones_like(hi)
        feats = jnp.stack([hi] * ALIBI_SPLIT + [lo] * ALIBI_SPLIT + [one] * ALIBI_SPLIT, axis=-1)
        feats = jnp.pad(feats, ((0, 0), (0, 0), (0, LANES - feats.shape[-1])))
        pieces = jnp.concatenate(_bf16_pieces(c), axis=-1).reshape(n_pair, 2 * ALIBI_SPLIT, 1)
        cvec = jnp.broadcast_to(jnp.pad(pieces, ((0, 0), (0, 8 - 2 * ALIBI_SPLIT), (0, 0))), (n_pair, 8, LANES))
        band = _diff_band(qn2, kn2, rel, c, t, vd.shape[3])
        return _diff_attention_call(qd, kd, vd, rel.reshape(b, nq, 1, t), feats, cvec, lam_params, sub,
                                    lambda_init, True, band)

    def general_path(_):
        pk = jnp.broadcast_to(positions[:, :, None], (b, s, LANES))
        cvec = jnp.broadcast_to(-c, (n_pair, 2, LANES))
        return _diff_attention_call(qd, kd, vd, positions.reshape(b, nq, 1, t), pk, cvec, lam_params, sub,
                                    lambda_init, False)

    return lax.cond(sorted_pos, sorted_path, general_path, None)


def _merge_kernel(h_ref, g_ref, ym_ref, yd_ref, pin_ref, halo_ref, wgate_ref, wb_ref, pw_ref, pb_ref, ps_ref,
                  wo_ref, o_ref, ext_ref, *, tm):
    i = pl.program_id(1)
    h = h_ref[0]
    d = h.shape[1]
    u = _rms(h, g_ref[...]).astype(BF16)

    x = pin_ref[0]
    ext_ref[0:POOL_HALO, :] = jnp.where(i == 0, jnp.zeros_like(halo_ref[0]), halo_ref[0])
    ext_ref[POOL_HALO:, :] = x
    tpos = i * tm + lax.broadcasted_iota(jnp.int32, (tm, POOL_GROUP), 0)
    yp = []
    for gi, w in enumerate(POOL_WINDOWS):
        ls = slice(gi * POOL_GROUP, (gi + 1) * POOL_GROUP)
        tot = ext_ref[POOL_HALO:POOL_HALO + tm, ls]
        for j in range(1, w):
            tot = tot + ext_ref[POOL_HALO - j:POOL_HALO - j + tm, ls]
        cnt = jnp.minimum(tpos + 1, w).astype(F32)
        pooled = tot / cnt - x[:, ls]
        yp.append((_dot(pooled.astype(BF16), pw_ref[gi]) + pb_ref[gi:gi + 1, :]) * ps_ref[:, ls])
    y_pool = jnp.concatenate(yp, axis=1).astype(BF16)

    merged = jnp.zeros((tm, d), F32)
    for bi, y in enumerate((ym_ref[0], yd_ref[0], y_pool)):
        gate = jax.nn.sigmoid(_dot(u, wgate_ref[:, bi * d:(bi + 1) * d]))
        merged = merged + gate * _dot(y, wb_ref[bi])
    o_ref[0] = h + _dot(merged.astype(BF16), wo_ref[...])


def _merge(h, g, y_mla, y_diff, pin, wts):
    b, s, d = h.shape
    tm = min(MERGE_TM, s)
    assert s % tm == 0 and tm % POOL_HALO == 0
    hb = tm // POOL_HALO
    consts = [wts[n] for n in ("wgate", "wb", "pw", "pb", "ps", "wo")]

    def cspec(a):
        return pl.BlockSpec(a.shape, lambda bi, i, _n=a.ndim: (0,) * _n)

    tile = lambda w: pl.BlockSpec((1, tm, w), lambda bi, i: (bi, i, 0))
    return pl.pallas_call(
        functools.partial(_merge_kernel, tm=tm),
        grid=(b, s // tm),
        in_specs=[tile(d), pl.BlockSpec((1, d), lambda bi, i: (0, 0)), tile(BRANCH_WIDTH), tile(BRANCH_WIDTH),
                  tile(512),
                  pl.BlockSpec((1, POOL_HALO, 512), lambda bi, i: (bi, jnp.maximum(i * hb - 1, 0), 0))]
                 + [cspec(a) for a in consts],
        out_specs=tile(d),
        out_shape=jax.ShapeDtypeStruct((b, s, d), F32),
        scratch_shapes=[pltpu.VMEM((tm + POOL_HALO, 512), F32)],
        compiler_params=_cparams(("parallel", "parallel")),
        name="gated_merge",
    )(h, g.reshape(1, d), y_mla, y_diff, pin, pin, *consts)


def _rot_half_cols(w):
    half = w.shape[-1] // 2
    return jnp.concatenate([-w[..., half:], w[..., :half]], axis=-1)


def _layer_weights(l, w_in, mla_q_norm, mla_w_uq, mla_kv_norm, mla_w_ukv, pool_w, pool_b, pool_scale,
                   w_branch, w_out):
    d = w_in.shape[1]
    win = w_in[l]
    o_kv = MLA_Q_RANK
    o_kr = o_kv + MLA_KV_RANK
    o_dq = o_kr + MLA_ROPE_DIM
    o_dk = o_dq + 512
    o_dv = o_dk + 512
    o_p = o_dv + 512
    o_g = o_p + 512
    w_kr = win[:, o_kr:o_dq]
    wn = jnp.concatenate([win[:, :o_dq], _rot_half_cols(w_kr), jnp.zeros((d, 64), F32),
                          win[:, o_dk:o_dv], win[:, o_p:o_g]], axis=1)
    wt = jnp.concatenate([win[:, o_dq:o_dk], win[:, o_dv:o_p]], axis=1).T

    wq = mla_w_uq[l].reshape(MLA_Q_RANK, MLA_HEADS, MLA_NOPE_DIM + MLA_ROPE_DIM)
    rope = wq[..., MLA_NOPE_DIM:]
    pad = jnp.zeros((MLA_Q_RANK, MLA_HEADS, MLA_HEAD_PAD - MLA_NOPE_DIM - MLA_ROPE_DIM), F32)
    w1 = jnp.concatenate([wq, pad], axis=-1).reshape(MLA_Q_RANK, MLA_HEADS * MLA_HEAD_PAD).T
    w2 = _rot_half_cols(rope).reshape(MLA_Q_RANK, MLA_HEADS * MLA_ROPE_DIM).T

    wkv = mla_w_ukv[l]
    lane = jnp.arange(wkv.shape[1]) % (MLA_NOPE_DIM + MLA_V_DIM)
    wkn = jnp.where(lane[None, :] < MLA_NOPE_DIM, wkv, 0.0)
    wv = wkv.reshape(MLA_KV_RANK, MLA_HEADS, MLA_NOPE_DIM + MLA_V_DIM)[..., MLA_NOPE_DIM:]
    wv = wv.reshape(MLA_KV_RANK, MLA_HEADS * MLA_V_DIM).T
    col = jnp.arange(MLA_HEADS * MLA_HEAD_PAD)
    e = ((col[None, :] % MLA_HEAD_PAD) == (MLA_NOPE_DIM + jnp.arange(MLA_ROPE_DIM)[:, None])).astype(BF16)

    bf = lambda a: a.astype(BF16)
    return {
        "wn": bf(wn), "wt": bf(wt), "qn": mla_q_norm[l].reshape(1, -1), "kvn": mla_kv_norm[l].reshape(1, -1),
        "w1": bf(w1), "w2": bf(w2), "wkn": bf(wkn), "e": e, "wv": bf(wv),
        "wgate": bf(win[:, o_g:]), "wb": bf(w_branch[l]), "pw": bf(pool_w[l]), "pb": pool_b[l],
        "ps": pool_scale[l].reshape(1, -1), "wo": bf(w_out[l]),
    }


def kernel(x, positions, ffn1_norm, ffn1_w_gate, ffn1_w_up, ffn1_w_down, mix_norm, w_in, mla_q_norm, mla_w_uq, mla_kv_norm, mla_w_ukv, diff_lambda_q1, diff_lambda_k1, diff_lambda_q2, diff_lambda_k2, diff_subln, pool_w, pool_b, pool_scale, w_branch, w_out, ffn2_norm, ffn2_w_gate, ffn2_w_up, ffn2_w_down, final_norm):
    b, s, d = x.shape
    depth = w_in.shape[0]
    tq, tk = min(ATT_TQ, s), min(ATT_TK, s)
    assert s % tq == 0

    half = MLA_ROPE_DIM // 2
    inv_freq = ROPE_BASE ** (-jnp.arange(half, dtype=F32) / half)
    tabs = _rope_tables(positions, jnp.concatenate([inv_freq, inv_freq]))
    slopes = jnp.exp2(-8.0 * jnp.arange(1, DIFF_HEADS + 1, dtype=F32) / DIFF_HEADS)

    bf = lambda a: a.astype(BF16)
    ffn1_w = (bf(ffn1_w_gate), bf(ffn1_w_up), bf(ffn1_w_down))
    ffn2_w = (bf(ffn2_w_gate), bf(ffn2_w_up), bf(ffn2_w_down))
    h = x.reshape(b * s, d)
    for l in range(depth):
        h = _ffn(h, ffn1_norm[l], *ffn1_w, l)
        wts = _layer_weights(l, w_in, mla_q_norm, mla_w_uq, mla_kv_norm, mla_w_ukv, pool_w, pool_b, pool_scale,
                             w_branch, w_out)
        h3 = h.reshape(b, s, d)
        qm, km, vm, qd, kd, vd, pin, qn2, kn2 = _proj(h3, mix_norm[l], wts, tabs, tq, tk)
        y_mla = _mla_attention(qm, km, vm)
        lambda_init = 0.8 - 0.6 * math.exp(-0.3 * l)
        lam_params = jnp.stack([diff_lambda_q1[l], diff_lambda_k1[l], diff_lambda_q2[l], diff_lambda_k2[l]])
        y_diff = _diff_attention(qd, kd, vd, qn2, kn2, positions, slopes, lam_params, diff_subln[l], lambda_init)
        h = _merge(h3, mix_norm[l], y_mla, y_diff, pin, wts).reshape(b * s, d)
        last = l == depth - 1
        h = _ffn(h, ffn2_norm[l], *ffn2_w, l, final_g=final_norm if last else None)
    return h.reshape(b, s, d)
```
